```python
import jax, jax.numpy as jnp
from jax import lax
import numpy as np

D_MODEL = 1024
BATCH = 8
SEQ = 4096
DEPTH = 1

PLE_DIM = 256
NORM_EPS = 1e-6
RW_HEADS = 8
RW_HEAD_DIM = 64
RW_WIDTH = RW_HEADS * RW_HEAD_DIM
DECAY_LORA = 64
AAA_LORA = 64
GATE_LORA = 160
RW_COLS = 3 * RW_WIDTH + DECAY_LORA + AAA_LORA + GATE_LORA
RW_LN_EPS = 64e-5
ATT_GROUPS = ((128, 1), (512, 4), (2048, 16))
ATT_HEADS_PER_GROUP = 4
ATT_HEADS = ATT_HEADS_PER_GROUP * len(ATT_GROUPS)
ATT_HEAD_DIM = 64
ATT_WIDTH = ATT_HEADS * ATT_HEAD_DIM
ATT_OUT = ATT_HEADS_PER_GROUP * ATT_HEAD_DIM
ATT_COLS = 3 * ATT_WIDTH
IN_COLS = RW_COLS + ATT_COLS
D_FF = 3 * D_MODEL
CONV_WIDTH = 3

kernel_name = "hybrid_rwkv7_dilated_alibi_convglu"


def rms_norm(x, g):
    xf = x.astype(jnp.float32)
    y = xf * lax.rsqrt(jnp.mean(xf * xf, axis=-1, keepdims=True) + NORM_EPS)
    return (y * g.astype(jnp.float32)).astype(x.dtype)


def shift_right(u, n):
    if n == 0:
        return u
    return jnp.pad(u, ((0, 0), (n, 0), (0, 0)))[:, :-n]


def alibi_slopes(n):
    return jnp.asarray(np.array([2.0 ** (-8.0 * (h + 1) / n) for h in range(n)], dtype=np.float32))


def rwkv7_time_mix(P, mu, w0, w_up, a0, a_up, g_up, k_k, k_a, r_k, ln_g, ln_b):
    B, T, _ = P.shape
    H, N = RW_HEADS, RW_HEAD_DIM
    Pm = P + (shift_right(P, 1) - P) * mu
    cuts = list(np.cumsum([RW_WIDTH, RW_WIDTH, RW_WIDTH, DECAY_LORA, AAA_LORA]))
    r, k, v, xw, xa, xg = jnp.split(Pm, cuts, axis=-1)
    w = -jax.nn.softplus(-(w0 + jnp.tanh(xw) @ w_up)) - 0.5
    decay = jnp.exp(-jnp.exp(w.astype(jnp.float32)))
    a = jax.nn.sigmoid(a0 + xa @ a_up)
    g = jax.nn.sigmoid(xg) @ g_up
    hs = lambda z: z.astype(jnp.float32).reshape(B, T, H, N)
    r, k, v, a, decay = hs(r), hs(k), hs(v), hs(a), hs(decay)
    kk = k * k_k.reshape(H, N)
    kk = kk / jnp.maximum(jnp.linalg.norm(kk, axis=-1, keepdims=True), 1e-12)
    k = k * (1.0 + (a - 1.0) * k_a.reshape(H, N))
    a_vec = -kk
    b_vec = kk * a

    def step(S, inp):
        r_t, w_t, k_t, v_t, a_t, b_t = inp
        Sa = jnp.einsum("bhvk,bhk->bhv", S, a_t)
        S = S * w_t[:, :, None, :] + Sa[..., None] * b_t[:, :, None, :] + v_t[..., None] * k_t[:, :, None, :]
        y = jnp.einsum("bhvk,bhk->bhv", S, r_t)
        return S, y

    tm = lambda z: jnp.swapaxes(z, 0, 1)
    S0 = jnp.zeros((B, H, N, N), jnp.float32)
    _, y = lax.scan(step, S0, (tm(r), tm(decay), tm(k), tm(v), tm(a_vec), tm(b_vec)))
    y = jnp.swapaxes(y, 0, 1)
    mean = jnp.mean(y, axis=-1, keepdims=True)
    var = jnp.mean(jnp.square(y - mean), axis=-1, keepdims=True)
    y = ((y - mean) * lax.rsqrt(var + RW_LN_EPS)).reshape(B, T, RW_WIDTH) * ln_g + ln_b
    bonus = (jnp.sum(r * k * r_k, axis=-1, keepdims=True) * v).reshape(B, T, RW_WIDTH)
    return ((y + bonus) * g).astype(P.dtype)


def dilated_group_attention(q, k, v, window, dilation, slopes):
    B, T, H, E = q.shape
    L = window // dilation
    span = L * dilation
    Tp = -(-T // span) * span
    pad = Tp - T
    nb = Tp // span

    def to_blocks(z):
        z = jnp.pad(z, ((0, 0), (0, pad), (0, 0), (0, 0)))
        return z.reshape(B, nb, L, dilation, H, E)

    def with_prev(z):
        zp = jnp.pad(z, ((0, 0), (1, 0), (0, 0), (0, 0), (0, 0), (0, 0)))[:, :-1]
        return jnp.concatenate([zp, z], axis=2)

    qb = to_blocks(q)
    kc = with_prev(to_blocks(k))
    vc = with_prev(to_blocks(v))
    s = jnp.einsum("bnqrhe,bnkrhe->bnrhqk", qb, kc).astype(jnp.float32) * (E ** -0.5)
    qi = jnp.arange(L)[:, None]
    kj = jnp.arange(2 * L)[None, :]
    steps = qi + L - kj
    blk = jnp.arange(nb)[:, None, None]
    valid = (steps >= 0) & (steps <= L) & (blk * L - L + kj >= 0)
    bias = -slopes[:, None, None] * (dilation * steps).astype(jnp.float32)[None]
    logits = jnp.where(valid[None, :, None, None], s + bias, -jnp.inf)
    lse = jax.nn.logsumexp(logits, axis=-1)
    prob = jnp.exp(logits - lse[..., None])
    o = jnp.einsum("bnrhqk,bnkrhe->bnqrhe", prob.astype(v.dtype), vc)
    o = o.reshape(B, Tp, H, E)[:, :T]
    lse = jnp.moveaxis(lse, 4, 2).reshape(B, Tp, H)[:, :T]
    return o, lse


def dilated_mixture_attention(P, slopes):
    B, T, _ = P.shape
    q, k, v = [z.reshape(B, T, ATT_HEADS, ATT_HEAD_DIM) for z in jnp.split(P, 3, axis=-1)]
    outs, lses = [], []
    for gi, (window, dilation) in enumerate(ATT_GROUPS):
        hsl = slice(gi * ATT_HEADS_PER_GROUP, (gi + 1) * ATT_HEADS_PER_GROUP)
        o, l = dilated_group_attention(q[:, :, hsl], k[:, :, hsl], v[:, :, hsl], window, dilation, slopes[hsl])
        outs.append(o.astype(jnp.float32))
        lses.append(l)
    wts = jax.nn.softmax(jnp.stack(lses, axis=0), axis=0)
    o = jnp.sum(wts[..., None] * jnp.stack(outs, axis=0), axis=0)
    return o.reshape(B, T, ATT_OUT).astype(P.dtype)


def conv_glu_ffn(h, w_up, conv_w, conv_b, w_down):
    u = h @ w_up
    u = conv_b + sum(conv_w[j] * shift_right(u, j) for j in range(CONV_WIDTH))
    gate, val = jnp.split(u, 2, axis=-1)
    return (jax.nn.gelu(gate, approximate=True) * val) @ w_down


def setup_inputs(seed: int = 0) -> dict:
    key = jax.random.key(seed)
    ks = jax.random.split(key, 32)
    Ld = DEPTH
    nrm = lambda kk, shape, fan: jax.random.normal(kk, shape, jnp.float32) * (fan ** -0.5)
    gain = lambda kk, n: 1.0 + 0.1 * jax.random.normal(kk, (Ld, n), jnp.float32)
    small = lambda kk, shape, s: s * jax.random.normal(kk, shape, jnp.float32)
    conv_w = jnp.array([1.0, 0.0, 0.0], jnp.float32)[None, :, None] + small(ks[22], (Ld, CONV_WIDTH, 2 * D_FF), 0.2)
    return {
        "x": jax.random.normal(ks[0], (BATCH, SEQ, D_MODEL), jnp.float32),
        "p": jax.random.normal(ks[1], (DEPTH, BATCH, SEQ, PLE_DIM), jnp.float32),
        "g_mix": gain(ks[2], D_MODEL),
        "w_in": nrm(ks[3], (Ld, D_MODEL, IN_COLS), D_MODEL),
        "rw_mu": jax.random.uniform(ks[4], (Ld, RW_COLS), jnp.float32),
        "rw_w0": jax.random.uniform(ks[5], (Ld, RW_WIDTH), jnp.float32, minval=-6.5, maxval=-1.5),
        "rw_w_up": 0.1 * nrm(ks[6], (Ld, DECAY_LORA, RW_WIDTH), DECAY_LORA),
        "rw_a0": small(ks[7], (Ld, RW_WIDTH), 0.1),
        "rw_a_up": nrm(ks[8], (Ld, AAA_LORA, RW_WIDTH), AAA_LORA),
        "rw_g_up": nrm(ks[9], (Ld, GATE_LORA, RW_WIDTH), GATE_LORA),
        "rw_k_k": 0.85 + small(ks[10], (Ld, RW_WIDTH), 0.05),
        "rw_k_a": 1.0 + small(ks[11], (Ld, RW_WIDTH), 0.05),
        "rw_r_k": small(ks[12], (Ld, RW_HEADS, RW_HEAD_DIM), 0.1),
        "rw_ln_g": gain(ks[13], RW_WIDTH),
        "rw_ln_b": small(ks[14], (Ld, RW_WIDTH), 0.01),
        "w_branch_a": nrm(ks[15], (Ld, RW_WIDTH, D_MODEL), RW_WIDTH),
        "w_branch_b": nrm(ks[16], (Ld, ATT_OUT, D_MODEL), ATT_OUT),
        "w_gate": nrm(ks[17], (Ld, D_MODEL, 2 * D_MODEL), D_MODEL),
        "b_gate": small(ks[18], (Ld, 2 * D_MODEL), 0.01),
        "w_out": nrm(ks[19], (Ld, D_MODEL, D_MODEL), D_MODEL),
        "g_ffn": gain(ks[20], D_MODEL),
        "w_up": nrm(ks[21], (Ld, D_MODEL, 2 * D_FF), D_MODEL),
        "conv_w": conv_w,
        "conv_b": small(ks[23], (Ld, 2 * D_FF), 0.01),
        "w_down": nrm(ks[24], (Ld, D_FF, D_MODEL), D_FF),
        "g_ple": gain(ks[25], D_MODEL),
        "w_ple_gate": nrm(ks[26], (Ld, D_MODEL, D_MODEL), D_MODEL),
        "w_ple": nrm(ks[27], (Ld, PLE_DIM, D_MODEL), PLE_DIM),
        "g_final": 1.0 + 0.1 * jax.random.normal(ks[28], (D_MODEL,), jnp.float32),
    }


def reference(x, p, g_mix, w_in, rw_mu, rw_w0, rw_w_up, rw_a0, rw_a_up, rw_g_up, rw_k_k, rw_k_a, rw_r_k,
              rw_ln_g, rw_ln_b, w_branch_a, w_branch_b, w_gate, b_gate, w_out, g_ffn, w_up, conv_w, conv_b,
              w_down, g_ple, w_ple_gate, w_ple, g_final):
    slopes = alibi_slopes(ATT_HEADS)
    for i in range(DEPTH):
        h = rms_norm(x, g_mix[i])
        proj = h @ w_in[i]
        y_a = rwkv7_time_mix(proj[..., :RW_COLS], rw_mu[i], rw_w0[i], rw_w_up[i], rw_a0[i], rw_a_up[i],
                             rw_g_up[i], rw_k_k[i], rw_k_a[i], rw_r_k[i], rw_ln_g[i], rw_ln_b[i])
        y_b = dilated_mixture_attention(proj[..., RW_COLS:], slopes)
        gate_a, gate_b = jnp.split(jax.nn.sigmoid(h @ w_gate[i] + b_gate[i]), 2, axis=-1)
        merged = gate_a * (y_a @ w_branch_a[i]) + gate_b * (y_b @ w_branch_b[i])
        x = x + merged @ w_out[i]
        x = x + conv_glu_ffn(rms_norm(x, g_ffn[i]), w_up[i], conv_w[i], conv_b[i], w_down[i])
        ple_gate = jax.nn.sigmoid(rms_norm(x, g_ple[i]) @ w_ple_gate[i])
        x = x + ple_gate * (p[i] @ w_ple[i])
    return rms_norm(x, g_final)
```

```python
from functools import partial

import numpy as np
import jax
import jax.numpy as jnp
from jax import lax
from jax.experimental import pallas as pl
from jax.experimental.pallas import tpu as pltpu

F32 = jnp.float32
BF16 = jnp.bfloat16

NORM_EPS = 1e-6
RW_LN_EPS = 64e-5
HEAD = 64
LANES = 128
CHUNK = 64
ATT_L = 128
ATT_GROUPS = ((128, 1), (512, 4), (2048, 16))
VMEM_LIMIT = 56 * 1024 * 1024

NN = (((1,), (0,)), ((), ()))
NT = (((1,), (1,)), ((), ()))


def _dot(a, b, dims=NN):
    return lax.dot_general(a, b, dims, preferred_element_type=F32)


def _split(x):
    hi = x.astype(BF16)
    lo = (x - hi.astype(F32)).astype(BF16)
    return hi, lo


def _dot3(a, b, dims=NN):
    ah, al = _split(a)
    bh, bl = _split(b)
    return (_dot(ah, bl, dims) + _dot(al, bh, dims)) + _dot(ah, bh, dims)


def _dot2_exact_rhs(a, b_bf16):
    ah, al = _split(a)
    return _dot(al, b_bf16) + _dot(ah, b_bf16)


def _rms(x, g):
    return x * lax.rsqrt(jnp.mean(x * x, axis=-1, keepdims=True) + NORM_EPS) * g


def _in_proj_kernel(x_ref, g_ref, wrw_ref, w1_ref, w2_ref, w3_ref, prw_ref, q1_ref, q2_ref, q3_ref):
    h = _rms(x_ref[...], g_ref[...]).astype(BF16)
    for w_ref, o_ref in ((wrw_ref, prw_ref), (w1_ref, q1_ref), (w2_ref, q2_ref), (w3_ref, q3_ref)):
        n = w_ref.shape[1]
        step = 384
        for c0 in range(0, n, step):
            o_ref[:, c0:c0 + step] = _dot(h, w_ref[:, c0:c0 + step])


def _in_proj(x2, g, wrw, watt, tm):
    n_tok, d = x2.shape
    const = lambda shape: pl.BlockSpec(shape, lambda i: (0, 0), pipeline_mode=pl.Buffered(1))
    tok = lambda n: pl.BlockSpec((tm, n), lambda i: (i, 0))
    return pl.pallas_call(
        _in_proj_kernel,
        grid=(n_tok // tm,),
        in_specs=[tok(d), const(g.shape), const(wrw.shape)] + [const(w.shape) for w in watt],
        out_specs=[tok(wrw.shape[1])] + [tok(w.shape[1]) for w in watt],
        out_shape=[jax.ShapeDtypeStruct((n_tok, wrw.shape[1]), F32)]
        + [jax.ShapeDtypeStruct((n_tok, w.shape[1]), F32) for w in watt],
        compiler_params=pltpu.CompilerParams(dimension_semantics=("arbitrary",), vmem_limit_bytes=VMEM_LIMIT),
        name="in_proj",
    )(x2, g, wrw, *watt)


def _stack(x, mask0):
    return jnp.concatenate([jnp.where(mask0, x, 0.0), jnp.where(mask0, 0.0, x)], axis=0)


def _pair_chunk(r, ld, k, v, av, bv, st, consts):
    ltri, mask0, strict, incl, eye = consts
    h1 = ld.astype(BF16)
    rem = ld - h1.astype(F32)
    h2 = rem.astype(BF16)
    h3 = (rem - h2.astype(F32)).astype(BF16)
    cum = (_dot(ltri, h3) + _dot(ltri, h2)) + _dot(ltri, h1)
    tot = cum[CHUNK - 1:CHUNK, :]
    w_in = jnp.exp(cum)
    w_ex = jnp.exp(cum - ld)
    w_inv = jnp.exp(-cum)
    w_end = jnp.exp(tot - cum)
    at = av * w_ex
    rt = r * w_in
    kt = k * w_inv
    bt = bv * w_inv
    kst, bst, vst = _stack(kt, mask0), _stack(bt, mask0), _stack(v, mask0)
    kwst, bwst = _stack(k * w_end, mask0), _stack(bv * w_end, mask0)

    a_ab = jnp.where(strict, _dot3(at, bst, NT), 0.0)
    a_ak = jnp.where(strict, _dot3(at, kst, NT), 0.0)
    a_rk = jnp.where(incl, _dot3(rt, kst, NT), 0.0)
    a_rb = jnp.where(incl, _dot3(rt, bst, NT), 0.0)

    n_bd = _stack(a_ab, mask0)
    tinv = eye + n_bd
    pw = n_bd
    for _ in range(5):
        pw = _dot3(pw, pw)
        tinv = tinv + _dot3(tinv, pw)

    abar = _dot3(tinv, _stack(at, mask0))
    u0 = _dot3(tinv, _stack(_dot3(a_ak, vst), mask0))
    rbar = rt + _dot3(a_rb, abar)
    y0 = _dot3(a_rk, vst) + _dot3(a_rb, u0)
    m_bd = jnp.where(eye > 0, jnp.exp(tot), 0.0) + _dot3(bwst.T, abar)
    g_bd = _dot3(kwst.T, vst) + _dot3(bwst.T, u0)

    y = _dot3(rbar, st) + y0
    return y, _dot3(m_bd, st) + g_bd


def _rwkv_kernel(p_ref, mu_ref, w0_ref, wlw_ref, wla_ref, a0_ref, wg_ref, kk_ref, ka_ref, rk_ref,
                 lng_ref, lnb_ref, o_ref, carry_ref, st_ref):
    tb = p_ref.shape[0]
    wd = o_ref.shape[1]
    npair = wd // LANES

    @pl.when(pl.program_id(1) == 0)
    def _():
        carry_ref[...] = jnp.zeros_like(carry_ref)
        st_ref[...] = jnp.zeros_like(st_ref)

    row = lax.broadcasted_iota(jnp.int32, (CHUNK, LANES), 0)
    lane = lax.broadcasted_iota(jnp.int32, (CHUNK, LANES), 1)
    mask0 = lane < HEAD
    src = lane & (HEAD - 1)
    r2 = lax.broadcasted_iota(jnp.int32, (2 * CHUNK, LANES), 0)
    l2 = lax.broadcasted_iota(jnp.int32, (2 * CHUNK, LANES), 1)
    eye = (r2 == l2).astype(F32)
    lr = lax.broadcasted_iota(jnp.int32, (CHUNK, CHUNK), 0)
    lc = lax.broadcasted_iota(jnp.int32, (CHUNK, CHUNK), 1)
    ltri = (lc <= lr).astype(BF16)
    consts = (ltri, mask0, src < row, src <= row, eye)
    e_pair = ((r2 < HEAD) == (l2 < HEAD)).astype(BF16)

    def headsum(x):
        return jnp.concatenate(
            [_dot2_exact_rhs(x[:, j * LANES:(j + 1) * LANES], e_pair) for j in range(npair)], axis=1)

    def body(c, _):
        r0 = pl.multiple_of(c * CHUNK, CHUNK)
        cur = p_ref[pl.ds(r0, CHUNK), :]
        prev = pltpu.roll(cur, 1, axis=0)
        prow = lax.broadcasted_iota(jnp.int32, cur.shape, 0)
        prev = jnp.where(prow == 0, carry_ref[...], prev)
        carry_ref[...] = cur[CHUNK - 1:CHUNK, :]
        pm = cur + (prev - cur) * mu_ref[...]

        r = pm[:, 0:wd]
        k = pm[:, wd:2 * wd]
        v = pm[:, 2 * wd:3 * wd]
        lora = pm[:, 3 * wd:3 * wd + LANES]
        xg = pm[:, 3 * wd + LANES:]
        lw = _dot(jnp.tanh(lora).astype(BF16), wlw_ref[...])
        la = _dot(lora.astype(BF16), wla_ref[...])
        g = _dot(jax.nn.sigmoid(xg).astype(BF16), wg_ref[...])
        z = w0_ref[...] + lw
        w = -(jnp.maximum(-z, 0.0) + jnp.log1p(jnp.exp(-jnp.abs(z)))) - 0.5
        ld = -jnp.exp(w)
        a = jax.nn.sigmoid(a0_ref[...] + la)
        kk = k * kk_ref[...]
        kk = kk / jnp.maximum(jnp.sqrt(headsum(kk * kk)), 1e-12)
        k2 = k * (1.0 + (a - 1.0) * ka_ref[...])
        bonus = headsum(r * k2 * rk_ref[...]) * v
        av = -kk
        bv = kk * a

        ys = []
        for j in range(npair):
            sl = slice(j * LANES, (j + 1) * LANES)
            y, st = _pair_chunk(r[:, sl], ld[:, sl], k2[:, sl], v[:, sl], av[:, sl], bv[:, sl],
                                st_ref[j], consts)
            st_ref[j] = st
            ys.append(y)
        y = jnp.concatenate(ys, axis=1)
        mean = headsum(y) * (1.0 / HEAD)
        yc = y - mean
        var = headsum(yc * yc) * (1.0 / HEAD)
        yn = yc * lax.rsqrt(var + RW_LN_EPS) * lng_ref[...] + lnb_ref[...]
        o_ref[pl.ds(r0, CHUNK), :] = (yn + bonus) * g
        return 0

    lax.fori_loop(0, tb // CHUNK, body, 0)


def _rwkv(prw, params, batch, seq, tb):
    n_tok, ncol = prw.shape
    wd = params[1].shape[1]
    nt = seq // tb
    const = lambda a: pl.BlockSpec(a.shape, lambda b, t: (0, 0), pipeline_mode=pl.Buffered(1))
    return pl.pallas_call(
        _rwkv_kernel,
        grid=(batch, nt),
        in_specs=[pl.BlockSpec((tb, ncol), lambda b, t: (b * nt + t, 0))] + [const(a) for a in params],
        out_specs=pl.BlockSpec((tb, wd), lambda b, t: (b * nt + t, 0)),
        out_shape=jax.ShapeDtypeStruct((n_tok, wd), F32),
        scratch_shapes=[pltpu.VMEM((1, ncol), F32), pltpu.VMEM((wd // LANES, LANES, LANES), F32)],
        compiler_params=pltpu.CompilerParams(dimension_semantics=("arbitrary", "arbitrary"),
                                             vmem_limit_bytes=VMEM_LIMIT),
        name="rwkv",
    )(prw, *params)


def _attn_kernel(q_ref, kc_ref, kp_ref, vc_ref, vp_ref, o_ref, l_ref, *, slopes):
    qb = q_ref.shape[1]
    n = pl.program_id(2)
    qi = lax.broadcasted_iota(jnp.int32, (ATT_L, 2 * ATT_L), 0)
    kj = lax.broadcasted_iota(jnp.int32, (ATT_L, 2 * ATT_L), 1)
    steps = qi + ATT_L - kj
    band = (steps >= 0) & (steps <= ATT_L)
    stepsf = steps.astype(F32)
    lane = lax.broadcasted_iota(jnp.int32, (ATT_L, LANES), 1)
    mask0 = lane < HEAD
    neg = jnp.float32(-1e30)
    for i in range(qb // ATT_L):
        rows = slice(i * ATT_L, (i + 1) * ATT_L)
        if i == 0:
            kprev, vprev = kp_ref[0], vp_ref[0]
            valid = band & (kj >= jnp.where(n > 0, 0, ATT_L))
        else:
            prows = slice((i - 1) * ATT_L, i * ATT_L)
            kprev, vprev = kc_ref[0, prows, :], vc_ref[0, prows, :]
            valid = band
        kcat = jnp.concatenate([kprev, kc_ref[0, rows, :]], axis=0).astype(BF16)
        vcat = jnp.concatenate([vprev, vc_ref[0, rows, :]], axis=0).astype(BF16)
        q = q_ref[0, rows, :]
        o_cols, l_cols = [], []
        for pr in range(q.shape[1] // LANES):
            cols = slice(pr * LANES, (pr + 1) * LANES)
            qp, kp, vp = q[:, cols], kcat[:, cols], vcat[:, cols]
            o_pair = jnp.zeros((ATT_L, LANES), F32)
            l_pair = jnp.zeros((ATT_L, LANES), F32)
            for hh in range(2):
                hmask = mask0 if hh == 0 else jnp.logical_not(mask0)
                qm = jnp.where(hmask, qp, 0.0).astype(BF16)
                s = _dot(qm, kp, NT) * (HEAD ** -0.5)
                logits = jnp.where(valid, s - slopes[2 * pr + hh] * stepsf, neg)
                m = jnp.max(logits, axis=1, keepdims=True)
                e = jnp.exp(logits - m)
                den = jnp.sum(e, axis=1, keepdims=True)
                o_h = _dot(e.astype(BF16), vp) / den
                o_pair = jnp.where(hmask, o_h, o_pair)
                l_pair = jnp.where(hmask, m + jnp.log(den), l_pair)
            o_cols.append(o_pair)
            l_cols.append(l_pair)
        o_ref[0, rows, :] = jnp.concatenate(o_cols, axis=1)
        l_ref[0, rows, :] = jnp.concatenate(l_cols, axis=1)


def _attn_group(qkv, batch, seq, dil, slopes):
    tq = seq // dil
    hw = qkv.shape[1] // 3
    x = qkv.reshape(batch, tq, dil * 3 * hw)
    qb = min(tq, 512)
    nq = qb // ATT_L
    spec = lambda part: pl.BlockSpec((1, qb, hw), lambda b, r, n: (b, n, 3 * r + part))
    prev = lambda part: pl.BlockSpec((1, ATT_L, hw), lambda b, r, n: (b, jnp.maximum(n * nq - 1, 0), 3 * r + part))
    out = pl.BlockSpec((1, qb, hw), lambda b, r, n: (b, n, r))
    o, l = pl.pallas_call(
        partial(_attn_kernel, slopes=tuple(float(s) * dil for s in slopes)),
        grid=(batch, dil, tq // qb),
        in_specs=[spec(0), spec(1), prev(1), spec(2), prev(2)],
        out_specs=[out, out],
        out_shape=[jax.ShapeDtypeStruct((batch, tq, dil * hw), F32)] * 2,
        compiler_params=pltpu.CompilerParams(dimension_semantics=("arbitrary",) * 3, vmem_limit_bytes=VMEM_LIMIT),
        name=f"attn_d{dil}",
    )(x, x, x, x, x)
    return o.reshape(batch * seq, hw), l.reshape(batch * seq, hw)


def _post_kernel(x_ref, ya_ref, o1_ref, l1_ref, o2_ref, l2_ref, o3_ref, l3_ref, p_ref,
                 gmix_ref, wgate_ref, bgate_ref, wa_ref, wb_ref, wout_ref, gffn_ref, wup_ref, cw_ref, cb_ref,
                 wdown_ref, gple_ref, wpg_ref, wple_ref, gfin_ref, out_ref, ucarry_ref, *, ff_chunk):
    tm, d = x_ref.shape
    dff = wdown_ref.shape[0]

    @pl.when(pl.program_id(1) == 0)
    def _():
        ucarry_ref[...] = jnp.zeros_like(ucarry_ref)

    x = x_ref[...]
    h = _rms(x, gmix_ref[...]).astype(BF16)
    gates = jax.nn.sigmoid(_dot(h, wgate_ref[...]) + bgate_ref[...])
    l1, l2, l3 = l1_ref[...], l2_ref[...], l3_ref[...]
    lm = jnp.maximum(jnp.maximum(l1, l2), l3)
    e1, e2, e3 = jnp.exp(l1 - lm), jnp.exp(l2 - lm), jnp.exp(l3 - lm)
    yb = (e1 * o1_ref[...] + e2 * o2_ref[...] + e3 * o3_ref[...]) / (e1 + e2 + e3)
    merged = (gates[:, :d] * _dot(ya_ref[...].astype(BF16), wa_ref[...])
              + gates[:, d:] * _dot(yb.astype(BF16), wb_ref[...]))
    x = x + _dot(merged.astype(BF16), wout_ref[...])

    h = _rms(x, gffn_ref[...]).astype(BF16)
    row = lax.broadcasted_iota(jnp.int32, (tm, ff_chunk), 0)

    def conv(u, cols):
        tail = ucarry_ref[:, cols]
        ucarry_ref[:, cols] = u[tm - 8:tm, :]
        s1 = jnp.where(row == 0, tail[7:8, :], pltpu.roll(u, 1, axis=0))
        s2 = jnp.where(row == 0, tail[6:7, :], jnp.where(row == 1, tail[7:8, :], pltpu.roll(u, 2, axis=0)))
        return cb_ref[:, cols] + cw_ref[0:1, cols] * u + cw_ref[1:2, cols] * s1 + cw_ref[2:3, cols] * s2

    acc = jnp.zeros((tm, d), F32)
    for c0 in range(0, dff, ff_chunk):
        gcols, vcols = slice(c0, c0 + ff_chunk), slice(dff + c0, dff + c0 + ff_chunk)
        gate = conv(_dot(h, wup_ref[:, gcols]), gcols)
        val = conv(_dot(h, wup_ref[:, vcols]), vcols)
        gelu = 0.5 * gate * (1.0 + jnp.tanh(np.sqrt(2.0 / np.pi).astype(np.float32) * (gate + 0.044715 * gate * gate * gate)))
        acc = acc + _dot((gelu * val).astype(BF16), wdown_ref[gcols, :])
    x = x + acc

    h = _rms(x, gple_ref[...]).astype(BF16)
    pg = jax.nn.sigmoid(_dot(h, wpg_ref[...]))
    x = x + pg * _dot(p_ref[...].astype(BF16), wple_ref[...])
    out_ref[...] = _rms(x, gfin_ref[...])


def _post(acts, weights, batch, seq, tm, ff_chunk):
    n_tok, d = acts[0].shape
    nt = seq // tm
    tok = lambda a: pl.BlockSpec((tm, a.shape[1]), lambda b, t: (b * nt + t, 0))
    const = lambda a: pl.BlockSpec(a.shape, lambda b, t: (0, 0), pipeline_mode=pl.Buffered(1))
    n_up = weights[7].shape[1]
    return pl.pallas_call(
        partial(_post_kernel, ff_chunk=ff_chunk),
        grid=(batch, nt),
        in_specs=[tok(a) for a in acts] + [const(w) for w in weights],
        out_specs=tok(acts[0]),
        out_shape=jax.ShapeDtypeStruct((n_tok, d), F32),
        scratch_shapes=[pltpu.VMEM((8, n_up), F32)],
        compiler_params=pltpu.CompilerParams(dimension_semantics=("arbitrary", "arbitrary"),
                                             vmem_limit_bytes=VMEM_LIMIT),
        name="post",
    )(*acts, *weights)


def _layer(x2, p2, batch, seq, g_mix, w_in, rw_mu, rw_w0, rw_w_up, rw_a0, rw_a_up, rw_g_up, rw_k_k, rw_k_a, rw_r_k,
           rw_ln_g, rw_ln_b, w_branch_a, w_branch_b, w_gate, b_gate, w_out, g_ffn, w_up, conv_w, conv_b,
           w_down, g_ple, w_ple_gate, w_ple, g_out):
    d = x2.shape[1]
    rw_w = rw_w0.shape[0]
    n_dec, n_aaa, n_gate = rw_w_up.shape[0], rw_a_up.shape[0], rw_g_up.shape[0]
    rw_cols = 3 * rw_w + n_dec + n_aaa + n_gate
    att_w = (w_in.shape[1] - rw_cols) // 3
    grp_w = att_w // len(ATT_GROUPS)
    assert n_dec + n_aaa == LANES and n_gate <= 2 * LANES
    row = lambda a: a.reshape(1, -1)

    gpad = 2 * LANES - n_gate
    wrw = jnp.pad(w_in[:, :rw_cols], ((0, 0), (0, gpad))).astype(BF16)
    mu = jnp.pad(rw_mu, (0, gpad)).reshape(1, -1)
    watt = []
    for gi in range(len(ATT_GROUPS)):
        cols = [w_in[:, rw_cols + part * att_w + gi * grp_w: rw_cols + part * att_w + (gi + 1) * grp_w] for part in range(3)]
        watt.append(jnp.concatenate(cols, axis=1).astype(BF16))
    wlw = jnp.pad(rw_w_up, ((0, n_aaa), (0, 0))).astype(BF16)
    wla = jnp.pad(rw_a_up, ((n_dec, 0), (0, 0))).astype(BF16)
    wg = jnp.pad(rw_g_up, ((0, gpad), (0, 0))).astype(BF16)

    prw, q1, q2, q3 = _in_proj(x2, row(g_mix), wrw, watt, tm=256)
    rw_params = (mu, row(rw_w0), wlw, wla, row(rw_a0), wg, row(rw_k_k), row(rw_k_a), row(rw_r_k),
                 row(rw_ln_g), row(rw_ln_b))
    ya = _rwkv(prw, rw_params, batch, seq, tb=512)

    n_heads = len(ATT_GROUPS) * grp_w // HEAD
    slopes = [2.0 ** (-8.0 * (h + 1) / n_heads) for h in range(n_heads)]
    att = []
    for gi, (qkv, (window, dil)) in enumerate(zip((q1, q2, q3), ATT_GROUPS)):
        assert window // dil == ATT_L
        hpg = grp_w // HEAD
        att.extend(_attn_group(qkv, batch, seq, dil, slopes[gi * hpg:(gi + 1) * hpg]))

    acts = (x2, ya, *att, p2)
    weights = (row(g_mix), w_gate.astype(BF16), row(b_gate), w_branch_a.astype(BF16), w_branch_b.astype(BF16),
               w_out.astype(BF16), row(g_ffn), w_up.astype(BF16), conv_w, row(conv_b), w_down.astype(BF16),
               row(g_ple), w_ple_gate.astype(BF16), w_ple.astype(BF16), row(g_out))
    return _post(acts, weights, batch, seq, tm=256, ff_chunk=512)


def kernel(x, p, g_mix, w_in, rw_mu, rw_w0, rw_w_up, rw_a0, rw_a_up, rw_g_up, rw_k_k, rw_k_a, rw_r_k, rw_ln_g, rw_ln_b, w_branch_a, w_branch_b, w_gate, b_gate, w_out, g_ffn, w_up, conv_w, conv_b, w_down, g_ple, w_ple_gate, w_ple, g_final):
    batch, seq, d = x.shape
    depth = w_in.shape[0]
    x2 = x.reshape(batch * seq, d)
    per_layer = (g_mix, w_in, rw_mu, rw_w0, rw_w_up, rw_a0, rw_a_up, rw_g_up, rw_k_k, rw_k_a, rw_r_k, rw_ln_g,
                 rw_ln_b, w_branch_a, w_branch_b, w_gate, b_gate, w_out, g_ffn, w_up, conv_w, conv_b, w_down,
                 g_ple, w_ple_gate, w_ple)
    for i in range(depth):
        assert depth == 1
        x2 = _layer(x2, p[i].reshape(batch * seq, -1), batch, seq, *[w[i] for w in per_layer], g_final)
    return x2.reshape(batch, seq, d)
```

```python
from functools import partial

import numpy as np
import jax
import jax.numpy as jnp
from jax import lax
from jax.experimental import pallas as pl
from jax.experimental.pallas import tpu as pltpu

F32 = jnp.float32
BF16 = jnp.bfloat16

NORM_EPS = 1e-6
RW_LN_EPS = 64e-5
HEAD = 64
LANES = 128
CHUNK = 64
ATT_L = 128
ATT_GROUPS = ((128, 1), (512, 4), (2048, 16))
VMEM_LIMIT = 56 * 1024 * 1024
TOKEN_TILE = 512
FF_CHUNK = 512

NN = (((1,), (0,)), ((), ()))
NT = (((1,), (1,)), ((), ()))


def _dot(a, b, dims=NN):
    return lax.dot_general(a, b, dims, preferred_element_type=F32)


def _split(x):
    hi = x.astype(BF16)
    lo = (x - hi.astype(F32)).astype(BF16)
    return hi, lo


def _dot3(a, b, dims=NN):
    ah, al = _split(a)
    bh, bl = _split(b)
    return (_dot(ah, bl, dims) + _dot(al, bh, dims)) + _dot(ah, bh, dims)


def _dot2_exact_rhs(a, b_bf16):
    ah, al = _split(a)
    return _dot(al, b_bf16) + _dot(ah, b_bf16)


def _rms(x, g):
    return x * lax.rsqrt(jnp.mean(x * x, axis=-1, keepdims=True) + NORM_EPS) * g


def _in_proj_kernel(x_ref, g_ref, wrw_ref, w1_ref, w2_ref, w3_ref, prw_ref, q1_ref, q2_ref, q3_ref):
    h = _rms(x_ref[...], g_ref[...]).astype(BF16)
    prw_ref[...] = _dot(h, wrw_ref[...])
    for w_ref, o_ref in ((w1_ref, q1_ref), (w2_ref, q2_ref), (w3_ref, q3_ref)):
        qkv = _dot(h, w_ref[...])
        for c in range(o_ref.shape[0]):
            o_ref[c] = qkv[:, c * LANES:(c + 1) * LANES]


def _in_proj(x2, g, wrw, watt, tm):
    n_tok, d = x2.shape
    const = lambda shape: pl.BlockSpec(shape, lambda i: (0, 0), pipeline_mode=pl.Buffered(1))
    tok = lambda n: pl.BlockSpec((tm, n), lambda i: (i, 0))
    slabs = [w.shape[1] // LANES for w in watt]
    return pl.pallas_call(
        _in_proj_kernel,
        grid=(n_tok // tm,),
        in_specs=[tok(d), const(g.shape), const(wrw.shape)] + [const(w.shape) for w in watt],
        out_specs=[tok(wrw.shape[1])] + [pl.BlockSpec((n, tm, LANES), lambda i: (0, i, 0)) for n in slabs],
        out_shape=[jax.ShapeDtypeStruct((n_tok, wrw.shape[1]), F32)]
        + [jax.ShapeDtypeStruct((n, n_tok, LANES), F32) for n in slabs],
        compiler_params=pltpu.CompilerParams(dimension_semantics=("arbitrary",), vmem_limit_bytes=VMEM_LIMIT),
        name="in_proj",
    )(x2, g, wrw, *watt)


def _stack(x, mask0):
    return jnp.concatenate([jnp.where(mask0, x, 0.0), jnp.where(mask0, 0.0, x)], axis=0)


def _chunk_maps(items, consts):
    ltri, mask0, strict, incl, eye = consts
    bf = lambda x: x.astype(BF16)
    each = lambda f, *cols: [f(*xs) for xs in zip(*cols)]
    stack = lambda x: _stack(x, mask0)
    r, ld, k, v, av, bv = (list(c) for c in zip(*items))

    h1 = each(bf, ld)
    rem = each(lambda x, h: x - h.astype(F32), ld, h1)
    h2 = each(bf, rem)
    h3 = each(lambda x, h: bf(x - h.astype(F32)), rem, h2)
    cum = each(lambda a, b, c: (_dot(ltri, c) + _dot(ltri, b)) + _dot(ltri, a), h1, h2, h3)
    tot = each(lambda c: c[CHUNK - 1:CHUNK, :], cum)
    at = each(lambda a, c, l: a * jnp.exp(c - l), av, cum, ld)
    rt = each(lambda x, c: x * jnp.exp(c), r, cum)
    w_inv = each(lambda c: jnp.exp(-c), cum)
    w_end = each(lambda t, c: jnp.exp(t - c), tot, cum)
    kst = each(lambda x, w: bf(stack(x * w)), k, w_inv)
    bst = each(lambda x, w: bf(stack(x * w)), bv, w_inv)
    vst = each(lambda x: bf(stack(x)), v)
    kwst_t = each(lambda x, w: bf(stack(x * w).T), k, w_end)
    bwst_t = each(lambda x, w: bf(stack(x * w).T), bv, w_end)

    lhs = each(lambda a, b: bf(jnp.concatenate([a, b], axis=0)), at, rt)
    res_b = each(lambda a, b: _dot(a, b, NT), lhs, bst)
    res_k = each(lambda a, b: _dot(a, b, NT), lhs, kst)
    a_ab = each(lambda x: jnp.where(strict, x[:CHUNK], 0.0), res_b)
    a_rb = each(lambda x: bf(jnp.where(incl, x[CHUNK:], 0.0)), res_b)
    a_ak = each(lambda x: bf(jnp.where(strict, x[:CHUNK], 0.0)), res_k)
    a_rk = each(lambda x: bf(jnp.where(incl, x[CHUNK:], 0.0)), res_k)

    n_bd = each(stack, a_ab)
    tinv = each(lambda x: eye + x, n_bd)
    pw = each(bf, n_bd)
    for _ in range(5):
        pw = each(lambda x: bf(_dot(x, x)), pw)
        tinv = each(lambda t, p: t + _dot(bf(t), p), tinv, pw)

    x0 = each(_dot, a_ak, vst)
    rhs = each(lambda a, x: bf(jnp.concatenate([stack(a), stack(x)], axis=1)), at, x0)
    z = each(lambda t, x: bf(_dot(bf(t), x)), tinv, rhs)
    ry = each(_dot, a_rb, z)
    mg = each(_dot, bwst_t, z)
    rbar = each(lambda a, b: a + b[:, :LANES], rt, ry)
    y0 = each(lambda a, x, b: _dot(a, x) + b[:, LANES:], a_rk, vst, ry)
    m_bd = each(lambda t, x: jnp.where(eye > 0, jnp.exp(t), 0.0) + x[:, :LANES], tot, mg)
    g_bd = each(lambda a, x, b: _dot(a, x) + b[:, LANES:], kwst_t, vst, mg)
    return list(zip(rbar, y0, m_bd, g_bd))


def _rwkv_kernel(p_ref, mu_ref, w0_ref, wlw_ref, wla_ref, a0_ref, wg_ref, kk_ref, ka_ref, rk_ref,
                 lng_ref, lnb_ref, o_ref, carry_ref, st_ref, *, chunks_per_step):
    tb = p_ref.shape[0]
    wd = o_ref.shape[1]
    npair = wd // LANES
    rows = chunks_per_step * CHUNK

    @pl.when(pl.program_id(1) == 0)
    def _():
        carry_ref[...] = jnp.zeros_like(carry_ref)
        st_ref[...] = jnp.zeros_like(st_ref)

    row = lax.broadcasted_iota(jnp.int32, (CHUNK, LANES), 0)
    lane = lax.broadcasted_iota(jnp.int32, (CHUNK, LANES), 1)
    mask0 = lane < HEAD
    src = lane & (HEAD - 1)
    r2 = lax.broadcasted_iota(jnp.int32, (2 * CHUNK, LANES), 0)
    l2 = lax.broadcasted_iota(jnp.int32, (2 * CHUNK, LANES), 1)
    eye = (r2 == l2).astype(F32)
    lr = lax.broadcasted_iota(jnp.int32, (CHUNK, CHUNK), 0)
    lc = lax.broadcasted_iota(jnp.int32, (CHUNK, CHUNK), 1)
    ltri = (lc <= lr).astype(BF16)
    consts = (ltri, mask0, src < row, src <= row, eye)
    e_pair = ((r2 < HEAD) == (l2 < HEAD)).astype(BF16)

    def headsum(x):
        return jnp.concatenate(
            [_dot2_exact_rhs(x[:, j * LANES:(j + 1) * LANES], e_pair) for j in range(npair)], axis=1)

    pair_cols = [slice(j * LANES, (j + 1) * LANES) for j in range(npair)]
    chunk_rows = [slice(i * CHUNK, (i + 1) * CHUNK) for i in range(chunks_per_step)]

    def body(c, _):
        r0 = pl.multiple_of(c * rows, rows)
        cur = p_ref[pl.ds(r0, rows), :]
        prev = pltpu.roll(cur, 1, axis=0)
        prow = lax.broadcasted_iota(jnp.int32, cur.shape, 0)
        prev = jnp.where(prow == 0, carry_ref[...], prev)
        carry_ref[...] = cur[rows - 1:rows, :]
        pm = cur + (prev - cur) * mu_ref[...]

        r = pm[:, 0:wd]
        k = pm[:, wd:2 * wd]
        v = pm[:, 2 * wd:3 * wd]
        lora = pm[:, 3 * wd:3 * wd + LANES]
        xg = pm[:, 3 * wd + LANES:]
        lw = _dot(jnp.tanh(lora).astype(BF16), wlw_ref[...])
        la = _dot(lora.astype(BF16), wla_ref[...])
        g = _dot(jax.nn.sigmoid(xg).astype(BF16), wg_ref[...])
        z = w0_ref[...] + lw
        w = -(jnp.maximum(-z, 0.0) + jnp.log1p(jnp.exp(-jnp.abs(z)))) - 0.5
        ld = -jnp.exp(w)
        a = jax.nn.sigmoid(a0_ref[...] + la)
        kk = k * kk_ref[...]
        kk = kk / jnp.maximum(jnp.sqrt(headsum(kk * kk)), 1e-12)
        k2 = k * (1.0 + (a - 1.0) * ka_ref[...])
        bonus = headsum(r * k2 * rk_ref[...]) * v
        av = -kk
        bv = kk * a

        items = [(r[rs, sl], ld[rs, sl], k2[rs, sl], v[rs, sl], av[rs, sl], bv[rs, sl])
                 for sl in pair_cols for rs in chunk_rows]
        maps = _chunk_maps(items, consts)
        cols = []
        for j in range(npair):
            st = st_ref[j]
            ys = []
            for rbar, y0, m_bd, g_bd in maps[j * chunks_per_step:(j + 1) * chunks_per_step]:
                ys.append(_dot3(rbar, st) + y0)
                st = _dot3(m_bd, st) + g_bd
            st_ref[j] = st
            cols.append(jnp.concatenate(ys, axis=0))
        y = jnp.concatenate(cols, axis=1)
        mean = headsum(y) * (1.0 / HEAD)
        yc = y - mean
        var = headsum(yc * yc) * (1.0 / HEAD)
        yn = yc * lax.rsqrt(var + RW_LN_EPS) * lng_ref[...] + lnb_ref[...]
        o_ref[pl.ds(r0, rows), :] = (yn + bonus) * g
        return 0

    lax.fori_loop(0, tb // rows, body, 0)


def _rwkv(prw, params, batch, seq, tb):
    n_tok, ncol = prw.shape
    wd = params[1].shape[1]
    nt = seq // tb
    const = lambda a: pl.BlockSpec(a.shape, lambda b, t: (0, 0), pipeline_mode=pl.Buffered(1))
    return pl.pallas_call(
        partial(_rwkv_kernel, chunks_per_step=2),
        grid=(batch, nt),
        in_specs=[pl.BlockSpec((tb, ncol), lambda b, t: (b * nt + t, 0))] + [const(a) for a in params],
        out_specs=pl.BlockSpec((tb, wd), lambda b, t: (b * nt + t, 0)),
        out_shape=jax.ShapeDtypeStruct((n_tok, wd), F32),
        scratch_shapes=[pltpu.VMEM((1, ncol), F32), pltpu.VMEM((wd // LANES, LANES, LANES), F32)],
        compiler_params=pltpu.CompilerParams(dimension_semantics=("arbitrary", "arbitrary"),
                                             vmem_limit_bytes=VMEM_LIMIT),
        name="rwkv",
    )(prw, *params)


def _attn_kernel(cur_ref, prev_ref, o_ref, l_ref, *, dil, slopes):
    qb = cur_ref.shape[1]
    npr = o_ref.shape[0]
    span = ATT_L * dil
    n = pl.program_id(1)
    qi = lax.broadcasted_iota(jnp.int32, (ATT_L, 2 * ATT_L), 0)
    kj = lax.broadcasted_iota(jnp.int32, (ATT_L, 2 * ATT_L), 1)
    steps = qi + ATT_L - kj
    band = (steps >= 0) & (steps <= ATT_L)
    first = band & (kj >= jnp.where(n > 0, 0, ATT_L))
    stepsf = steps.astype(F32)
    lane = lax.broadcasted_iota(jnp.int32, (ATT_L, LANES), 1)
    mask0 = lane < HEAD
    neg = jnp.float32(-1e30)
    for s in range(qb // span):
        for r in range(dil):
            sel = lambda base: pl.ds(base + r, ATT_L, stride=dil) if dil > 1 else pl.ds(base, ATT_L)
            rows = sel(s * span)
            for pr in range(npr):
                if s == 0:
                    kprev, vprev = prev_ref[npr + pr, sel(0), :], prev_ref[2 * npr + pr, sel(0), :]
                    valid = first
                else:
                    prows = sel((s - 1) * span)
                    kprev, vprev = cur_ref[npr + pr, prows, :], cur_ref[2 * npr + pr, prows, :]
                    valid = band
                kp = jnp.concatenate([kprev, cur_ref[npr + pr, rows, :]], axis=0).astype(BF16)
                vp = jnp.concatenate([vprev, cur_ref[2 * npr + pr, rows, :]], axis=0).astype(BF16)
                qp = cur_ref[pr, rows, :]
                o_pair = jnp.zeros((ATT_L, LANES), F32)
                l_pair = jnp.zeros((ATT_L, LANES), F32)
                for hh in range(2):
                    hmask = mask0 if hh == 0 else jnp.logical_not(mask0)
                    qm = jnp.where(hmask, qp, 0.0).astype(BF16)
                    sc = _dot(qm, kp, NT) * (HEAD ** -0.5)
                    logits = jnp.where(valid, sc - slopes[2 * pr + hh] * stepsf, neg)
                    m = jnp.max(logits, axis=1, keepdims=True)
                    e = jnp.exp(logits - m)
                    den = jnp.sum(e, axis=1, keepdims=True)
                    o_h = _dot(e.astype(BF16), vp) / den
                    o_pair = jnp.where(hmask, o_h, o_pair)
                    l_pair = jnp.where(hmask, m + jnp.log(den), l_pair)
                o_ref[pr, rows, :] = o_pair
                l_ref[pr, rows, :] = l_pair


def _attn_group(qkv, batch, seq, dil, slopes):
    nsl, n_tok, _ = qkv.shape
    npr = nsl // 3
    span = ATT_L * dil
    qb = max(span, 512)
    nt = seq // qb
    per = qb // span
    cur = pl.BlockSpec((nsl, qb, LANES), lambda b, n: (0, b * nt + n, 0))
    prev = pl.BlockSpec((nsl, span, LANES), lambda b, n: (0, b * (seq // span) + jnp.maximum(n * per - 1, 0), 0))
    out = pl.BlockSpec((npr, qb, LANES), lambda b, n: (0, b * nt + n, 0))
    return pl.pallas_call(
        partial(_attn_kernel, dil=dil, slopes=tuple(float(s) * dil for s in slopes)),
        grid=(batch, nt),
        in_specs=[cur, prev],
        out_specs=[out, out],
        out_shape=[jax.ShapeDtypeStruct((npr, n_tok, LANES), F32)] * 2,
        compiler_params=pltpu.CompilerParams(dimension_semantics=("arbitrary",) * 2, vmem_limit_bytes=VMEM_LIMIT),
        name=f"attn_d{dil}",
    )(qkv, qkv)


def _merge_kernel(x_ref, ya_ref, o1_ref, l1_ref, o2_ref, l2_ref, o3_ref, l3_ref,
                  gmix_ref, wgate_ref, bgate_ref, wa_ref, wb_ref, wout_ref, out_ref):
    d = x_ref.shape[1]
    x = x_ref[...]
    h = _rms(x, gmix_ref[...]).astype(BF16)
    gates = jax.nn.sigmoid(_dot(h, wgate_ref[...]) + bgate_ref[...])
    cat = lambda ref: jnp.concatenate([ref[c] for c in range(ref.shape[0])], axis=1)
    l1, l2, l3 = cat(l1_ref), cat(l2_ref), cat(l3_ref)
    lm = jnp.maximum(jnp.maximum(l1, l2), l3)
    e1, e2, e3 = jnp.exp(l1 - lm), jnp.exp(l2 - lm), jnp.exp(l3 - lm)
    yb = (e1 * cat(o1_ref) + e2 * cat(o2_ref) + e3 * cat(o3_ref)) / (e1 + e2 + e3)
    merged = (gates[:, :d] * _dot(ya_ref[...].astype(BF16), wa_ref[...])
              + gates[:, d:] * _dot(yb.astype(BF16), wb_ref[...]))
    out_ref[...] = x + _dot(merged.astype(BF16), wout_ref[...])


def _merge(x2, ya, att, weights, tm):
    n_tok, d = x2.shape
    tok = lambda a: pl.BlockSpec((tm, a.shape[1]), lambda i: (i, 0))
    slab = lambda a: pl.BlockSpec((a.shape[0], tm, LANES), lambda i: (0, i, 0))
    const = lambda a: pl.BlockSpec(a.shape, lambda i: (0, 0), pipeline_mode=pl.Buffered(1))
    return pl.pallas_call(
        _merge_kernel,
        grid=(n_tok // tm,),
        in_specs=[tok(x2), tok(ya)] + [slab(a) for a in att] + [const(w) for w in weights],
        out_specs=tok(x2),
        out_shape=jax.ShapeDtypeStruct((n_tok, d), F32),
        compiler_params=pltpu.CompilerParams(dimension_semantics=("arbitrary",), vmem_limit_bytes=VMEM_LIMIT),
        name="merge",
    )(x2, ya, *att, *weights)


def _ffn_kernel(x_ref, p_ref, gffn_ref, wup_ref, cw_ref, cb_ref, wdown_ref, gple_ref, wpg_ref, wple_ref, gfin_ref,
                out_ref, ucarry_ref, *, ff_chunk):
    tm, d = x_ref.shape
    dff = wdown_ref.shape[0]

    @pl.when(pl.program_id(1) == 0)
    def _():
        ucarry_ref[...] = jnp.zeros_like(ucarry_ref)

    x = x_ref[...]
    h = _rms(x, gffn_ref[...]).astype(BF16)
    row8 = lax.broadcasted_iota(jnp.int32, (8, ff_chunk), 0)

    def conv(u, cols):
        tail = ucarry_ref[:, cols]
        ucarry_ref[:, cols] = u[tm - 8:tm, :]
        r1, r2 = pltpu.roll(u, 1, axis=0), pltpu.roll(u, 2, axis=0)
        top1 = jnp.where(row8 == 0, tail[7:8, :], r1[:8])
        top2 = jnp.where(row8 == 0, tail[6:7, :], jnp.where(row8 == 1, tail[7:8, :], r2[:8]))
        s1 = jnp.concatenate([top1, r1[8:]], axis=0)
        s2 = jnp.concatenate([top2, r2[8:]], axis=0)
        return cb_ref[:, cols] + cw_ref[0:1, cols] * u + cw_ref[1:2, cols] * s1 + cw_ref[2:3, cols] * s2

    def up(c0):
        return _dot(h, wup_ref[:, c0:c0 + ff_chunk]), _dot(h, wup_ref[:, dff + c0:dff + c0 + ff_chunk])

    c1 = np.float32(np.sqrt(2.0 / np.pi))
    c3 = np.float32(np.sqrt(2.0 / np.pi) * 0.044715)
    acc = jnp.zeros((tm, d), F32)
    starts = list(range(0, dff, ff_chunk))
    nxt = up(starts[0])
    for i, c0 in enumerate(starts):
        ug, uv = nxt
        if i + 1 < len(starts):
            nxt = up(starts[i + 1])
        gcols = slice(c0, c0 + ff_chunk)
        gate = conv(ug, gcols)
        val = conv(uv, slice(dff + c0, dff + c0 + ff_chunk))
        hv = 0.5 * gate * val
        act = hv + hv * jnp.tanh(gate * (c1 + c3 * (gate * gate)))
        acc = acc + _dot(act.astype(BF16), wdown_ref[gcols, :])
    x = x + acc

    h = _rms(x, gple_ref[...]).astype(BF16)
    pg = jax.nn.sigmoid(_dot(h, wpg_ref[...]))
    x = x + pg * _dot(p_ref[...].astype(BF16), wple_ref[...])
    out_ref[...] = _rms(x, gfin_ref[...])


def _ffn(x2, p2, weights, batch, seq, tm, ff_chunk):
    n_tok, d = x2.shape
    nt = seq // tm
    tok = lambda a: pl.BlockSpec((tm, a.shape[1]), lambda b, t: (b * nt + t, 0))
    const = lambda a: pl.BlockSpec(a.shape, lambda b, t: (0, 0), pipeline_mode=pl.Buffered(1))
    n_up = weights[1].shape[1]
    return pl.pallas_call(
        partial(_ffn_kernel, ff_chunk=ff_chunk),
        grid=(batch, nt),
        in_specs=[tok(x2), tok(p2)] + [const(w) for w in weights],
        out_specs=tok(x2),
        out_shape=jax.ShapeDtypeStruct((n_tok, d), F32),
        scratch_shapes=[pltpu.VMEM((8, n_up), F32)],
        compiler_params=pltpu.CompilerParams(dimension_semantics=("arbitrary", "arbitrary"),
                                             vmem_limit_bytes=VMEM_LIMIT),
        name="ffn",
    )(x2, p2, *weights)


def _layer(x2, p2, batch, seq, g_mix, w_in, rw_mu, rw_w0, rw_w_up, rw_a0, rw_a_up, rw_g_up, rw_k_k, rw_k_a, rw_r_k,
           rw_ln_g, rw_ln_b, w_branch_a, w_branch_b, w_gate, b_gate, w_out, g_ffn, w_up, conv_w, conv_b,
           w_down, g_ple, w_ple_gate, w_ple, g_out):
    d = x2.shape[1]
    rw_w = rw_w0.shape[0]
    n_dec, n_aaa, n_gate = rw_w_up.shape[0], rw_a_up.shape[0], rw_g_up.shape[0]
    rw_cols = 3 * rw_w + n_dec + n_aaa + n_gate
    att_w = (w_in.shape[1] - rw_cols) // 3
    grp_w = att_w // len(ATT_GROUPS)
    assert n_dec + n_aaa == LANES and n_gate <= 2 * LANES
    row = lambda a: a.reshape(1, -1)

    gpad = 2 * LANES - n_gate
    wrw = jnp.pad(w_in[:, :rw_cols], ((0, 0), (0, gpad))).astype(BF16)
    mu = jnp.pad(rw_mu, (0, gpad)).reshape(1, -1)
    watt = []
    for gi in range(len(ATT_GROUPS)):
        cols = [w_in[:, rw_cols + part * att_w + gi * grp_w: rw_cols + part * att_w + (gi + 1) * grp_w] for part in range(3)]
        watt.append(jnp.concatenate(cols, axis=1).astype(BF16))
    wlw = jnp.pad(rw_w_up, ((0, n_aaa), (0, 0))).astype(BF16)
    wla = jnp.pad(rw_a_up, ((n_dec, 0), (0, 0))).astype(BF16)
    wg = jnp.pad(rw_g_up, ((0, gpad), (0, 0))).astype(BF16)

    prw, q1, q2, q3 = _in_proj(x2, row(g_mix), wrw, watt, tm=TOKEN_TILE)
    rw_params = (mu, row(rw_w0), wlw, wla, row(rw_a0), wg, row(rw_k_k), row(rw_k_a), row(rw_r_k),
                 row(rw_ln_g), row(rw_ln_b))
    ya = _rwkv(prw, rw_params, batch, seq, tb=TOKEN_TILE)

    n_heads = len(ATT_GROUPS) * grp_w // HEAD
    slopes = [2.0 ** (-8.0 * (h + 1) / n_heads) for h in range(n_heads)]
    att = []
    for gi, (qkv, (window, dil)) in enumerate(zip((q1, q2, q3), ATT_GROUPS)):
        assert window // dil == ATT_L
        hpg = grp_w // HEAD
        att.extend(_attn_group(qkv, batch, seq, dil, slopes[gi * hpg:(gi + 1) * hpg]))

    merge_w = (row(g_mix), w_gate.astype(BF16), row(b_gate), w_branch_a.astype(BF16), w_branch_b.astype(BF16),
               w_out.astype(BF16))
    x2 = _merge(x2, ya, att, merge_w, tm=TOKEN_TILE)
    ffn_w = (row(g_ffn), w_up.astype(BF16), conv_w, row(conv_b), w_down.astype(BF16),
             row(g_ple), w_ple_gate.astype(BF16), w_ple.astype(BF16), row(g_out))
    return _ffn(x2, p2, ffn_w, batch, seq, tm=TOKEN_TILE, ff_chunk=FF_CHUNK)


def kernel(x, p, g_mix, w_in, rw_mu, rw_w0, rw_w_up, rw_a0, rw_a_up, rw_g_up, rw_k_k, rw_k_a, rw_r_k, rw_ln_g, rw_ln_b, w_branch_a, w_branch_b, w_gate, b_gate, w_out, g_ffn, w_up, conv_w, conv_b, w_down, g_ple, w_ple_gate, w_ple, g_final):
    batch, seq, d = x.shape
    depth = w_in.shape[0]
    x2 = x.reshape(batch * seq, d)
    per_layer = (g_mix, w_in, rw_mu, rw_w0, rw_w_up, rw_a0, rw_a_up, rw_g_up, rw_k_k, rw_k_a, rw_r_k, rw_ln_g,
                 rw_ln_b, w_branch_a, w_branch_b, w_gate, b_gate, w_out, g_ffn, w_up, conv_w, conv_b, w_down,
                 g_ple, w_ple_gate, w_ple)
    for i in range(depth):
        assert depth == 1
        x2 = _layer(x2, p[i].reshape(batch * seq, -1), batch, seq, *[w[i] for w in per_layer], g_final)
    return x2.reshape(batch, seq, d)
```

```python
from functools import partial

import numpy as np
import jax
import jax.numpy as jnp
from jax import lax
from jax.experimental import pallas as pl
from jax.experimental.pallas import tpu as pltpu

F32 = jnp.float32
BF16 = jnp.bfloat16

NORM_EPS = 1e-6
RW_LN_EPS = 64e-5
HEAD = 64
LANES = 128
CHUNK = 64
ATT_L = 128
ATT_GROUPS = ((128, 1), (512, 4), (2048, 16))
VMEM_LIMIT = 56 * 1024 * 1024
TOKEN_TILE = 512
FF_CHUNK = 1024
RW_CHUNKS_PER_STEP = 4

NN = (((1,), (0,)), ((), ()))
NT = (((1,), (1,)), ((), ()))


def _dot(a, b, dims=NN):
    return lax.dot_general(a, b, dims, preferred_element_type=F32)


def _split(x):
    hi = x.astype(BF16)
    lo = (x - hi.astype(F32)).astype(BF16)
    return hi, lo


def _dot2_exact_rhs(a, b_bf16):
    ah, al = _split(a)
    return _dot(al, b_bf16) + _dot(ah, b_bf16)


def _rms(x, g):
    return x * lax.rsqrt(jnp.mean(x * x, axis=-1, keepdims=True) + NORM_EPS) * g


def _in_proj_kernel(x_ref, g_ref, wrw_ref, w1_ref, w2_ref, w3_ref, prw_ref, q1_ref, q2_ref, q3_ref):
    h = _rms(x_ref[...], g_ref[...]).astype(BF16)
    prw_ref[...] = _dot(h, wrw_ref[...])
    for w_ref, o_ref in ((w1_ref, q1_ref), (w2_ref, q2_ref), (w3_ref, q3_ref)):
        qkv = _dot(h, w_ref[...])
        for c in range(o_ref.shape[0]):
            o_ref[c] = qkv[:, c * LANES:(c + 1) * LANES]


def _in_proj(x2, g, wrw, watt, tm):
    n_tok, d = x2.shape
    const = lambda shape: pl.BlockSpec(shape, lambda i: (0, 0), pipeline_mode=pl.Buffered(1))
    tok = lambda n: pl.BlockSpec((tm, n), lambda i: (i, 0))
    slabs = [w.shape[1] // LANES for w in watt]
    return pl.pallas_call(
        _in_proj_kernel,
        grid=(n_tok // tm,),
        in_specs=[tok(d), const(g.shape), const(wrw.shape)] + [const(w.shape) for w in watt],
        out_specs=[tok(wrw.shape[1])] + [pl.BlockSpec((n, tm, LANES), lambda i: (0, i, 0)) for n in slabs],
        out_shape=[jax.ShapeDtypeStruct((n_tok, wrw.shape[1]), F32)]
        + [jax.ShapeDtypeStruct((n, n_tok, LANES), F32) for n in slabs],
        compiler_params=pltpu.CompilerParams(dimension_semantics=("arbitrary",), vmem_limit_bytes=VMEM_LIMIT),
        name="in_proj",
    )(x2, g, wrw, *watt)


def _stack(x, mask0):
    return jnp.concatenate([jnp.where(mask0, x, 0.0), jnp.where(mask0, 0.0, x)], axis=0)


def _chunk_maps(items, consts, hooks=()):
    ltri, mask0, strict, incl, eye = consts
    bf = lambda x: x.astype(BF16)
    each = lambda f, *cols: [f(*xs) for xs in zip(*cols)]
    stack = lambda x: _stack(x, mask0)
    r, ld, k, v, av, bv = (list(c) for c in zip(*items))
    hooks = list(hooks)
    run_hook = lambda: hooks.pop(0)() if hooks else None

    h1 = each(bf, ld)
    rem = each(lambda x, h: x - h.astype(F32), ld, h1)
    h2 = each(bf, rem)
    h3 = each(lambda x, h: bf(x - h.astype(F32)), rem, h2)
    cum = each(lambda a, b, c: (_dot(ltri, c) + _dot(ltri, b)) + _dot(ltri, a), h1, h2, h3)
    tot = each(lambda c: c[CHUNK - 1:CHUNK, :], cum)
    run_hook()
    at = each(lambda a, c, l: a * jnp.exp(c - l), av, cum, ld)
    rt = each(lambda x, c: x * jnp.exp(c), r, cum)
    w_inv = each(lambda c: jnp.exp(-c), cum)
    w_end = each(lambda t, c: jnp.exp(t - c), tot, cum)
    run_hook()
    kst = each(lambda x, w: bf(stack(x * w)), k, w_inv)
    bst = each(lambda x, w: bf(stack(x * w)), bv, w_inv)
    vst_f = each(stack, v)
    vst = each(bf, vst_f)
    vst_t = each(lambda x: bf(x.T), vst_f)
    run_hook()
    kwst = each(lambda x, w: bf(stack(x * w)), k, w_end)
    bwst = each(lambda x, w: bf(stack(x * w)), bv, w_end)

    lhs = each(lambda a, b: bf(jnp.concatenate([a, b], axis=0)), at, rt)
    res_b = each(lambda a, b: _dot(a, b, NT), lhs, bst)
    res_k = each(lambda a, b: _dot(a, b, NT), lhs, kst)
    run_hook()
    a_ab = each(lambda x: jnp.where(strict, x[:CHUNK], 0.0), res_b)
    a_rb = each(lambda x: bf(jnp.where(incl, x[CHUNK:], 0.0)), res_b)
    a_ak = each(lambda x: bf(jnp.where(strict, x[:CHUNK], 0.0)), res_k)
    a_rk = each(lambda x: bf(jnp.where(incl, x[CHUNK:], 0.0)), res_k)

    n_bd = each(stack, a_ab)
    tinv = each(lambda x: eye + x, n_bd)
    pw = each(bf, n_bd)
    while hooks:
        run_hook()
    for _ in range(5):
        pw = each(lambda x: bf(_dot(x, x)), pw)
        tinv = each(lambda t, p: t + _dot(bf(t), p), tinv, pw)

    x0 = each(_dot, a_ak, vst)
    rhs = each(lambda a, x: bf(jnp.concatenate([stack(a), stack(x)], axis=1)), at, x0)
    zf = each(lambda t, x: _dot(bf(t), x), tinv, rhs)
    z = each(bf, zf)
    z_t = each(lambda x: bf(jnp.concatenate([x[:, :LANES].T, x[:, LANES:].T], axis=0)), zf)
    ry = each(_dot, a_rb, z)
    mg = each(_dot, z_t, bwst)
    rbar = each(lambda a, b: a + b[:, :LANES], rt, ry)
    y0 = each(lambda a, x, b: _dot(a, x) + b[:, LANES:], a_rk, vst, ry)
    mt = each(lambda t, x: jnp.where(eye > 0, jnp.exp(t), 0.0) + x[:LANES], tot, mg)
    gt = each(lambda a, x, b: _dot(a, x) + b[LANES:], vst_t, kwst, mg)
    return list(zip(rbar, y0, mt, gt))


def _rwkv_kernel(p_ref, mu_ref, w0_ref, wlw_ref, wla_ref, a0_ref, wg_ref, kk_ref, ka_ref, rk_ref,
                 lng_ref, lnb_ref, o_ref, carry_ref, st_ref, *, chunks_per_step):
    tb = p_ref.shape[0]
    wd = o_ref.shape[1]
    npair = wd // LANES
    rows = chunks_per_step * CHUNK

    @pl.when(pl.program_id(1) == 0)
    def _():
        carry_ref[...] = jnp.zeros_like(carry_ref)
        st_ref[...] = jnp.zeros_like(st_ref)

    row = lax.broadcasted_iota(jnp.int32, (CHUNK, LANES), 0)
    lane = lax.broadcasted_iota(jnp.int32, (CHUNK, LANES), 1)
    mask0 = lane < HEAD
    src = lane & (HEAD - 1)
    r2 = lax.broadcasted_iota(jnp.int32, (2 * CHUNK, LANES), 0)
    l2 = lax.broadcasted_iota(jnp.int32, (2 * CHUNK, LANES), 1)
    eye = (r2 == l2).astype(F32)
    lr = lax.broadcasted_iota(jnp.int32, (CHUNK, CHUNK), 0)
    lc = lax.broadcasted_iota(jnp.int32, (CHUNK, CHUNK), 1)
    ltri = (lc <= lr).astype(BF16)
    consts = (ltri, mask0, src < row, src <= row, eye)
    e_pair = ((r2 < HEAD) == (l2 < HEAD)).astype(BF16)

    def headsum(x):
        return jnp.concatenate(
            [_dot2_exact_rhs(x[:, j * LANES:(j + 1) * LANES], e_pair) for j in range(npair)], axis=1)

    pair_cols = [slice(j * LANES, (j + 1) * LANES) for j in range(npair)]
    chunk_rows = [slice(i * CHUNK, (i + 1) * CHUNK) for i in range(chunks_per_step)]

    def prepare(r0):
        cur = p_ref[r0:r0 + rows, :]
        prev = pltpu.roll(cur, 1, axis=0)
        prow = lax.broadcasted_iota(jnp.int32, cur.shape, 0)
        prev = jnp.where(prow == 0, carry_ref[...], prev)
        carry_ref[...] = cur[rows - 1:rows, :]
        pm = cur + (prev - cur) * mu_ref[...]

        r = pm[:, 0:wd]
        k = pm[:, wd:2 * wd]
        v = pm[:, 2 * wd:3 * wd]
        lora = pm[:, 3 * wd:3 * wd + LANES]
        xg = pm[:, 3 * wd + LANES:]
        lw = _dot(jnp.tanh(lora).astype(BF16), wlw_ref[...])
        la = _dot(lora.astype(BF16), wla_ref[...])
        g = _dot(jax.nn.sigmoid(xg).astype(BF16), wg_ref[...])
        z = w0_ref[...] + lw
        w = -(jnp.maximum(-z, 0.0) + jnp.log1p(jnp.exp(-jnp.abs(z)))) - 0.5
        ld = -jnp.exp(w)
        a = jax.nn.sigmoid(a0_ref[...] + la)
        kk = k * kk_ref[...]
        kk = kk / jnp.maximum(jnp.sqrt(headsum(kk * kk)), 1e-12)
        k2 = k * (1.0 + (a - 1.0) * ka_ref[...])
        bonus = headsum(r * k2 * rk_ref[...]) * v
        av = -kk
        bv = kk * a

        items = [(r[rs, sl], ld[rs, sl], k2[rs, sl], v[rs, sl], av[rs, sl], bv[rs, sl])
                 for sl in pair_cols for rs in chunk_rows]
        return items, bonus, g

    def state_steps(maps, bonus, g, r0):
        bf = lambda x: x.astype(BF16)
        sts, ys = [None] * npair, [[] for _ in range(npair)]

        def chunk_step(i):
            for j in range(npair):
                rbar, y0, mt, gt = maps[j * chunks_per_step + i]
                s_hi, s_lo = _split(st_ref[j] if i == 0 else sts[j])
                ys[j].append(_dot(bf(rbar), s_hi, NT) + y0)
                mtb = bf(mt)
                sts[j] = (_dot(s_lo, mtb) + _dot(s_hi, mtb)) + gt
                if i == chunks_per_step - 1:
                    st_ref[j] = sts[j]

        def output():
            y = jnp.concatenate([jnp.concatenate(c, axis=0) for c in ys], axis=1)
            mean = headsum(y) * (1.0 / HEAD)
            yc = y - mean
            var = headsum(yc * yc) * (1.0 / HEAD)
            yn = yc * lax.rsqrt(var + RW_LN_EPS) * lng_ref[...] + lnb_ref[...]
            o_ref[r0:r0 + rows, :] = (yn + bonus) * g

        return [partial(chunk_step, i) for i in range(chunks_per_step)] + [output]

    pending = []
    for r0 in range(0, tb, rows):
        items, bonus, g = prepare(r0)
        maps = _chunk_maps(items, consts, hooks=pending)
        pending = state_steps(maps, bonus, g, r0)
    for step in pending:
        step()


def _rwkv(prw, params, batch, seq, tb):
    n_tok, ncol = prw.shape
    wd = params[1].shape[1]
    nt = seq // tb
    const = lambda a: pl.BlockSpec(a.shape, lambda b, t: (0, 0), pipeline_mode=pl.Buffered(1))
    return pl.pallas_call(
        partial(_rwkv_kernel, chunks_per_step=RW_CHUNKS_PER_STEP),
        grid=(batch, nt),
        in_specs=[pl.BlockSpec((tb, ncol), lambda b, t: (b * nt + t, 0))] + [const(a) for a in params],
        out_specs=pl.BlockSpec((tb, wd), lambda b, t: (b * nt + t, 0)),
        out_shape=jax.ShapeDtypeStruct((n_tok, wd), F32),
        scratch_shapes=[pltpu.VMEM((1, ncol), F32), pltpu.VMEM((wd // LANES, LANES, LANES), F32)],
        compiler_params=pltpu.CompilerParams(dimension_semantics=("arbitrary", "arbitrary"),
                                             vmem_limit_bytes=VMEM_LIMIT),
        name="rwkv",
    )(prw, *params)


def _attn_kernel(cur_ref, prev_ref, o_ref, l_ref, *, dil, slopes):
    qb = cur_ref.shape[1]
    npr = o_ref.shape[0]
    span = ATT_L * dil
    n = pl.program_id(1)
    qi = lax.broadcasted_iota(jnp.int32, (ATT_L, 2 * ATT_L), 0)
    kj = lax.broadcasted_iota(jnp.int32, (ATT_L, 2 * ATT_L), 1)
    steps = qi + ATT_L - kj
    band = (steps >= 0) & (steps <= ATT_L)
    first = band & (kj >= jnp.where(n > 0, 0, ATT_L))
    stepsf = steps.astype(F32)
    lane = lax.broadcasted_iota(jnp.int32, (ATT_L, LANES), 1)
    mask0 = lane < HEAD
    neg = jnp.float32(-1e30)
    for s in range(qb // span):
        for r in range(dil):
            sel = lambda base: pl.ds(base + r, ATT_L, stride=dil) if dil > 1 else pl.ds(base, ATT_L)
            rows = sel(s * span)
            for pr in range(npr):
                if s == 0:
                    kprev, vprev = prev_ref[npr + pr, sel(0), :], prev_ref[2 * npr + pr, sel(0), :]
                    valid = first
                else:
                    prows = sel((s - 1) * span)
                    kprev, vprev = cur_ref[npr + pr, prows, :], cur_ref[2 * npr + pr, prows, :]
                    valid = band
                kp = jnp.concatenate([kprev, cur_ref[npr + pr, rows, :]], axis=0).astype(BF16)
                vp = jnp.concatenate([vprev, cur_ref[2 * npr + pr, rows, :]], axis=0).astype(BF16)
                qp = cur_ref[pr, rows, :]
                o_pair = jnp.zeros((ATT_L, LANES), F32)
                l_pair = jnp.zeros((ATT_L, LANES), F32)
                for hh in range(2):
                    hmask = mask0 if hh == 0 else jnp.logical_not(mask0)
                    qm = jnp.where(hmask, qp, 0.0).astype(BF16)
                    sc = _dot(qm, kp, NT) * (HEAD ** -0.5)
                    logits = jnp.where(valid, sc - slopes[2 * pr + hh] * stepsf, neg)
                    m = jnp.max(logits, axis=1, keepdims=True)
                    e = jnp.exp(logits - m)
                    den = jnp.sum(e, axis=1, keepdims=True)
                    o_h = _dot(e.astype(BF16), vp) / den
                    o_pair = jnp.where(hmask, o_h, o_pair)
                    l_pair = jnp.where(hmask, m + jnp.log(den), l_pair)
                o_ref[pr, rows, :] = o_pair
                l_ref[pr, rows, :] = l_pair


def _attn_group(qkv, batch, seq, dil, slopes):
    nsl, n_tok, _ = qkv.shape
    npr = nsl // 3
    span = ATT_L * dil
    qb = max(span, 512)
    nt = seq // qb
    per = qb // span
    cur = pl.BlockSpec((nsl, qb, LANES), lambda b, n: (0, b * nt + n, 0))
    prev = pl.BlockSpec((nsl, span, LANES), lambda b, n: (0, b * (seq // span) + jnp.maximum(n * per - 1, 0), 0))
    out = pl.BlockSpec((npr, qb, LANES), lambda b, n: (0, b * nt + n, 0))
    return pl.pallas_call(
        partial(_attn_kernel, dil=dil, slopes=tuple(float(s) * dil for s in slopes)),
        grid=(batch, nt),
        in_specs=[cur, prev],
        out_specs=[out, out],
        out_shape=[jax.ShapeDtypeStruct((npr, n_tok, LANES), F32)] * 2,
        compiler_params=pltpu.CompilerParams(dimension_semantics=("arbitrary",) * 2, vmem_limit_bytes=VMEM_LIMIT),
        name=f"attn_d{dil}",
    )(qkv, qkv)


def _merge_kernel(x_ref, ya_ref, o1_ref, l1_ref, o2_ref, l2_ref, o3_ref, l3_ref,
                  gmix_ref, wgate_ref, bgate_ref, wa_ref, wb_ref, wout_ref, out_ref):
    d = x_ref.shape[1]
    x = x_ref[...]
    h = _rms(x, gmix_ref[...]).astype(BF16)
    gates = jax.nn.sigmoid(_dot(h, wgate_ref[...]) + bgate_ref[...])
    cat = lambda ref: jnp.concatenate([ref[c] for c in range(ref.shape[0])], axis=1)
    l1, l2, l3 = cat(l1_ref), cat(l2_ref), cat(l3_ref)
    lm = jnp.maximum(jnp.maximum(l1, l2), l3)
    e1, e2, e3 = jnp.exp(l1 - lm), jnp.exp(l2 - lm), jnp.exp(l3 - lm)
    yb = (e1 * cat(o1_ref) + e2 * cat(o2_ref) + e3 * cat(o3_ref)) / (e1 + e2 + e3)
    merged = (gates[:, :d] * _dot(ya_ref[...].astype(BF16), wa_ref[...])
              + gates[:, d:] * _dot(yb.astype(BF16), wb_ref[...]))
    out_ref[...] = x + _dot(merged.astype(BF16), wout_ref[...])


def _merge(x2, ya, att, weights, tm):
    n_tok, d = x2.shape
    tok = lambda a: pl.BlockSpec((tm, a.shape[1]), lambda i: (i, 0))
    slab = lambda a: pl.BlockSpec((a.shape[0], tm, LANES), lambda i: (0, i, 0))
    const = lambda a: pl.BlockSpec(a.shape, lambda i: (0, 0), pipeline_mode=pl.Buffered(1))
    return pl.pallas_call(
        _merge_kernel,
        grid=(n_tok // tm,),
        in_specs=[tok(x2), tok(ya)] + [slab(a) for a in att] + [const(w) for w in weights],
        out_specs=tok(x2),
        out_shape=jax.ShapeDtypeStruct((n_tok, d), F32),
        compiler_params=pltpu.CompilerParams(dimension_semantics=("arbitrary",), vmem_limit_bytes=VMEM_LIMIT),
        name="merge",
    )(x2, ya, *att, *weights)


def _ffn_kernel(x_ref, p_ref, gffn_ref, wup_ref, cw_ref, cb_ref, wdown_ref, gple_ref, wpg_ref, wple_ref, gfin_ref,
                out_ref, ucarry_ref, *, ff_chunk):
    tm, d = x_ref.shape
    dff = wdown_ref.shape[0]

    @pl.when(pl.program_id(1) == 0)
    def _():
        ucarry_ref[...] = jnp.zeros_like(ucarry_ref)

    x = x_ref[...]
    h = _rms(x, gffn_ref[...]).astype(BF16)
    row8 = lax.broadcasted_iota(jnp.int32, (8, ff_chunk), 0)

    def conv(u, cols):
        tail = ucarry_ref[:, cols]
        ucarry_ref[:, cols] = u[tm - 8:tm, :]
        r1, r2 = pltpu.roll(u, 1, axis=0), pltpu.roll(u, 2, axis=0)
        top1 = jnp.where(row8 == 0, tail[7:8, :], r1[:8])
        top2 = jnp.where(row8 == 0, tail[6:7, :], jnp.where(row8 == 1, tail[7:8, :], r2[:8]))
        s1 = jnp.concatenate([top1, r1[8:]], axis=0)
        s2 = jnp.concatenate([top2, r2[8:]], axis=0)
        return cb_ref[:, cols] + cw_ref[0:1, cols] * u + cw_ref[1:2, cols] * s1 + cw_ref[2:3, cols] * s2

    def up(c0):
        return _dot(h, wup_ref[:, c0:c0 + ff_chunk]), _dot(h, wup_ref[:, dff + c0:dff + c0 + ff_chunk])

    c1 = np.float32(np.sqrt(2.0 / np.pi))
    c3 = np.float32(np.sqrt(2.0 / np.pi) * 0.044715)
    acc = jnp.zeros((tm, d), F32)
    starts = list(range(0, dff, ff_chunk))
    nxt = up(starts[0])
    for i, c0 in enumerate(starts):
        ug, uv = nxt
        if i + 1 < len(starts):
            nxt = up(starts[i + 1])
        gcols = slice(c0, c0 + ff_chunk)
        gate = conv(ug, gcols)
        val = conv(uv, slice(dff + c0, dff + c0 + ff_chunk))
        hv = 0.5 * gate * val
        act = hv + hv * jnp.tanh(gate * (c1 + c3 * (gate * gate)))
        acc = acc + _dot(act.astype(BF16), wdown_ref[gcols, :])
    x = x + acc

    h = _rms(x, gple_ref[...]).astype(BF16)
    pg = jax.nn.sigmoid(_dot(h, wpg_ref[...]))
    x = x + pg * _dot(p_ref[...].astype(BF16), wple_ref[...])
    out_ref[...] = _rms(x, gfin_ref[...])


def _ffn(x2, p2, weights, batch, seq, tm, ff_chunk):
    n_tok, d = x2.shape
    nt = seq // tm
    tok = lambda a: pl.BlockSpec((tm, a.shape[1]), lambda b, t: (b * nt + t, 0))
    const = lambda a: pl.BlockSpec(a.shape, lambda b, t: (0, 0), pipeline_mode=pl.Buffered(1))
    n_up = weights[1].shape[1]
    return pl.pallas_call(
        partial(_ffn_kernel, ff_chunk=ff_chunk),
        grid=(batch, nt),
        in_specs=[tok(x2), tok(p2)] + [const(w) for w in weights],
        out_specs=tok(x2),
        out_shape=jax.ShapeDtypeStruct((n_tok, d), F32),
        scratch_shapes=[pltpu.VMEM((8, n_up), F32)],
        compiler_params=pltpu.CompilerParams(dimension_semantics=("arbitrary", "arbitrary"),
                                             vmem_limit_bytes=VMEM_LIMIT),
        name="ffn",
    )(x2, p2, *weights)


def _layer(x2, p2, batch, seq, g_mix, w_in, rw_mu, rw_w0, rw_w_up, rw_a0, rw_a_up, rw_g_up, rw_k_k, rw_k_a, rw_r_k,
           rw_ln_g, rw_ln_b, w_branch_a, w_branch_b, w_gate, b_gate, w_out, g_ffn, w_up, conv_w, conv_b,
           w_down, g_ple, w_ple_gate, w_ple, g_out):
    d = x2.shape[1]
    rw_w = rw_w0.shape[0]
    n_dec, n_aaa, n_gate = rw_w_up.shape[0], rw_a_up.shape[0], rw_g_up.shape[0]
    rw_cols = 3 * rw_w + n_dec + n_aaa + n_gate
    att_w = (w_in.shape[1] - rw_cols) // 3
    grp_w = att_w // len(ATT_GROUPS)
    assert n_dec + n_aaa == LANES and n_gate <= 2 * LANES
    row = lambda a: a.reshape(1, -1)

    gpad = 2 * LANES - n_gate
    wrw = jnp.pad(w_in[:, :rw_cols], ((0, 0), (0, gpad))).astype(BF16)
    mu = jnp.pad(rw_mu, (0, gpad)).reshape(1, -1)
    watt = []
    for gi in range(len(ATT_GROUPS)):
        cols = [w_in[:, rw_cols + part * att_w + gi * grp_w: rw_cols + part * att_w + (gi + 1) * grp_w] for part in range(3)]
        watt.append(jnp.concatenate(cols, axis=1).astype(BF16))
    wlw = jnp.pad(rw_w_up, ((0, n_aaa), (0, 0))).astype(BF16)
    wla = jnp.pad(rw_a_up, ((n_dec, 0), (0, 0))).astype(BF16)
    wg = jnp.pad(rw_g_up, ((0, gpad), (0, 0))).astype(BF16)

    prw, q1, q2, q3 = _in_proj(x2, row(g_mix), wrw, watt, tm=TOKEN_TILE)
    rw_params = (mu, row(rw_w0), wlw, wla, row(rw_a0), wg, row(rw_k_k), row(rw_k_a), row(rw_r_k),
                 row(rw_ln_g), row(rw_ln_b))
    ya = _rwkv(prw, rw_params, batch, seq, tb=TOKEN_TILE)

    n_heads = len(ATT_GROUPS) * grp_w // HEAD
    slopes = [2.0 ** (-8.0 * (h + 1) / n_heads) for h in range(n_heads)]
    att = []
    for gi, (qkv, (window, dil)) in enumerate(zip((q1, q2, q3), ATT_GROUPS)):
        assert window // dil == ATT_L
        hpg = grp_w // HEAD
        att.extend(_attn_group(qkv, batch, seq, dil, slopes[gi * hpg:(gi + 1) * hpg]))

    merge_w = (row(g_mix), w_gate.astype(BF16), row(b_gate), w_branch_a.astype(BF16), w_branch_b.astype(BF16),
               w_out.astype(BF16))
    x2 = _merge(x2, ya, att, merge_w, tm=TOKEN_TILE)
    ffn_w = (row(g_ffn), w_up.astype(BF16), conv_w, row(conv_b), w_down.astype(BF16),
             row(g_ple), w_ple_gate.astype(BF16), w_ple.astype(BF16), row(g_out))
    return _ffn(x2, p2, ffn_w, batch, seq, tm=TOKEN_TILE, ff_chunk=FF_CHUNK)


def kernel(x, p, g_mix, w_in, rw_mu, rw_w0, rw_w_up, rw_a0, rw_a_up, rw_g_up, rw_k_k, rw_k_a, rw_r_k, rw_ln_g, rw_ln_b, w_branch_a, w_branch_b, w_gate, b_gate, w_out, g_ffn, w_up, conv_w, conv_b, w_down, g_ple, w_ple_gate, w_ple, g_final):
    batch, seq, d = x.shape
    depth = w_in.shape[0]
    x2 = x.reshape(batch * seq, d)
    per_layer = (g_mix, w_in, rw_mu, rw_w0, rw_w_up, rw_a0, rw_a_up, rw_g_up, rw_k_k, rw_k_a, rw_r_k, rw_ln_g,
                 rw_ln_b, w_branch_a, w_branch_b, w_gate, b_gate, w_out, g_ffn, w_up, conv_w, conv_b, w_down,
                 g_ple, w_ple_gate, w_ple)
    for i in range(depth):
        assert depth == 1
        x2 = _layer(x2, p[i].reshape(batch * seq, -1), batch, seq, *[w[i] for w in per_layer], g_final)
    return x2.reshape(batch, seq, d)
```

```python
from functools import partial

import numpy as np
import jax
import jax.numpy as jnp
from jax import lax
from jax.experimental import pallas as pl
from jax.experimental.pallas import tpu as pltpu

F32 = jnp.float32
BF16 = jnp.bfloat16

NORM_EPS = 1e-6
RW_LN_EPS = 64e-5
HEAD = 64
LANES = 128
CHUNK = 64
ATT_L = 128
ATT_GROUPS = ((128, 1), (512, 4), (2048, 16))
VMEM_LIMIT = 56 * 1024 * 1024
TOKEN_TILE = 512
FF_CHUNK = 1024
RW_CHUNKS_PER_STEP = 4
RW_TOKEN_TILE = 512

NN = (((1,), (0,)), ((), ()))
NT = (((1,), (1,)), ((), ()))


def _dot(a, b, dims=NN):
    return lax.dot_general(a, b, dims, preferred_element_type=F32)


def _split(x):
    hi = x.astype(BF16)
    lo = (x - hi.astype(F32)).astype(BF16)
    return hi, lo


def _dot2_exact_rhs(a, b_bf16):
    ah, al = _split(a)
    return _dot(al, b_bf16) + _dot(ah, b_bf16)


def _rms(x, g):
    return x * lax.rsqrt(jnp.mean(x * x, axis=-1, keepdims=True) + NORM_EPS) * g


def _in_proj_kernel(x_ref, g_ref, wrw_ref, w1_ref, w2_ref, w3_ref, prw_ref, q1_ref, q2_ref, q3_ref):
    h = _rms(x_ref[...], g_ref[...]).astype(BF16)
    prw_ref[...] = _dot(h, wrw_ref[...])
    for w_ref, o_ref in ((w1_ref, q1_ref), (w2_ref, q2_ref), (w3_ref, q3_ref)):
        qkv = _dot(h, w_ref[...])
        for c in range(o_ref.shape[0]):
            o_ref[c] = qkv[:, c * LANES:(c + 1) * LANES]


def _in_proj(x2, g, wrw, watt, tm):
    n_tok, d = x2.shape
    const = lambda shape: pl.BlockSpec(shape, lambda i: (0, 0), pipeline_mode=pl.Buffered(1))
    tok = lambda n: pl.BlockSpec((tm, n), lambda i: (i, 0))
    slabs = [w.shape[1] // LANES for w in watt]
    return pl.pallas_call(
        _in_proj_kernel,
        grid=(n_tok // tm,),
        in_specs=[tok(d), const(g.shape), const(wrw.shape)] + [const(w.shape) for w in watt],
        out_specs=[tok(wrw.shape[1])] + [pl.BlockSpec((n, tm, LANES), lambda i: (0, i, 0)) for n in slabs],
        out_shape=[jax.ShapeDtypeStruct((n_tok, wrw.shape[1]), F32)]
        + [jax.ShapeDtypeStruct((n, n_tok, LANES), F32) for n in slabs],
        compiler_params=pltpu.CompilerParams(dimension_semantics=("arbitrary",), vmem_limit_bytes=VMEM_LIMIT),
        name="in_proj",
    )(x2, g, wrw, *watt)


def _stack(x, mask0):
    return jnp.concatenate([jnp.where(mask0, x, 0.0), jnp.where(mask0, 0.0, x)], axis=0)


def _chunk_maps(items, consts, hooks=()):
    ltri, mask0, strict, incl, eye = consts
    bf = lambda x: x.astype(BF16)
    each = lambda f, *cols: [f(*xs) for xs in zip(*cols)]
    stack = lambda x: _stack(x, mask0)
    r, ld, k, v, av, bv = (list(c) for c in zip(*items))
    hooks = list(hooks)
    run_hook = lambda: hooks.pop(0)() if hooks else None

    h1 = each(bf, ld)
    rem = each(lambda x, h: x - h.astype(F32), ld, h1)
    h2 = each(bf, rem)
    h3 = each(lambda x, h: bf(x - h.astype(F32)), rem, h2)
    cum = each(lambda a, b, c: (_dot(ltri, c) + _dot(ltri, b)) + _dot(ltri, a), h1, h2, h3)
    tot = each(lambda c: c[CHUNK - 1:CHUNK, :], cum)
    run_hook()
    at = each(lambda a, c, l: a * jnp.exp(c - l), av, cum, ld)
    rt = each(lambda x, c: x * jnp.exp(c), r, cum)
    w_inv = each(lambda c: jnp.exp(-c), cum)
    w_end = each(lambda t, c: jnp.exp(t - c), tot, cum)
    run_hook()
    kst = each(lambda x, w: bf(stack(x * w)), k, w_inv)
    bst = each(lambda x, w: bf(stack(x * w)), bv, w_inv)
    vst_f = each(stack, v)
    vst = each(bf, vst_f)
    vst_t = each(lambda x: bf(x.T), vst_f)
    run_hook()
    kwst = each(lambda x, w: bf(stack(x * w)), k, w_end)
    bwst = each(lambda x, w: bf(stack(x * w)), bv, w_end)

    lhs = each(lambda a, b: bf(jnp.concatenate([a, b], axis=0)), at, rt)
    res_b = each(lambda a, b: _dot(a, b, NT), lhs, bst)
    res_k = each(lambda a, b: _dot(a, b, NT), lhs, kst)
    run_hook()
    a_ab = each(lambda x: jnp.where(strict, x[:CHUNK], 0.0), res_b)
    a_rb = each(lambda x: bf(jnp.where(incl, x[CHUNK:], 0.0)), res_b)
    a_ak = each(lambda x: bf(jnp.where(strict, x[:CHUNK], 0.0)), res_k)
    a_rk = each(lambda x: bf(jnp.where(incl, x[CHUNK:], 0.0)), res_k)

    n_bd = each(stack, a_ab)
    tinv = each(lambda x: eye + x, n_bd)
    pw = each(bf, n_bd)
    while hooks:
        run_hook()
    for _ in range(5):
        pw = each(lambda x: bf(_dot(x, x)), pw)
        tinv = each(lambda t, p: t + _dot(bf(t), p), tinv, pw)

    x0 = each(_dot, a_ak, vst)
    rhs = each(lambda a, x: bf(jnp.concatenate([stack(a), stack(x)], axis=1)), at, x0)
    zf = each(lambda t, x: _dot(bf(t), x), tinv, rhs)
    z = each(bf, zf)
    z_t = each(lambda x: bf(jnp.concatenate([x[:, :LANES].T, x[:, LANES:].T], axis=0)), zf)
    ry = each(_dot, a_rb, z)
    mg = each(_dot, z_t, bwst)
    rbar = each(lambda a, b: a + b[:, :LANES], rt, ry)
    y0 = each(lambda a, x, b: _dot(a, x) + b[:, LANES:], a_rk, vst, ry)
    mt = each(lambda t, x: jnp.where(eye > 0, jnp.exp(t), 0.0) + x[:LANES], tot, mg)
    gt = each(lambda a, x, b: _dot(a, x) + b[LANES:], vst_t, kwst, mg)
    return list(zip(rbar, y0, mt, gt))


def _rwkv_kernel(p_ref, mu_ref, w0_ref, wlw_ref, wla_ref, a0_ref, wg_ref, kk_ref, ka_ref, rk_ref,
                 lng_ref, lnb_ref, o_ref, carry_ref, st_ref, *, chunks_per_step):
    tb = p_ref.shape[0]
    wd = o_ref.shape[1]
    npair = wd // LANES
    rows = chunks_per_step * CHUNK

    @pl.when(pl.program_id(1) == 0)
    def _():
        carry_ref[...] = jnp.zeros_like(carry_ref)
        st_ref[...] = jnp.zeros_like(st_ref)

    row = lax.broadcasted_iota(jnp.int32, (CHUNK, LANES), 0)
    lane = lax.broadcasted_iota(jnp.int32, (CHUNK, LANES), 1)
    mask0 = lane < HEAD
    src = lane & (HEAD - 1)
    r2 = lax.broadcasted_iota(jnp.int32, (2 * CHUNK, LANES), 0)
    l2 = lax.broadcasted_iota(jnp.int32, (2 * CHUNK, LANES), 1)
    eye = (r2 == l2).astype(F32)
    lr = lax.broadcasted_iota(jnp.int32, (CHUNK, CHUNK), 0)
    lc = lax.broadcasted_iota(jnp.int32, (CHUNK, CHUNK), 1)
    ltri = (lc <= lr).astype(BF16)
    consts = (ltri, mask0, src < row, src <= row, eye)
    e_pair = ((r2 < HEAD) == (l2 < HEAD)).astype(BF16)

    def headsum(x):
        return jnp.concatenate(
            [_dot2_exact_rhs(x[:, j * LANES:(j + 1) * LANES], e_pair) for j in range(npair)], axis=1)

    pair_cols = [slice(j * LANES, (j + 1) * LANES) for j in range(npair)]
    chunk_rows = [slice(i * CHUNK, (i + 1) * CHUNK) for i in range(chunks_per_step)]

    def prepare(r0):
        cur = p_ref[r0:r0 + rows, :]
        prev = pltpu.roll(cur, 1, axis=0)
        prow = lax.broadcasted_iota(jnp.int32, cur.shape, 0)
        prev = jnp.where(prow == 0, carry_ref[...], prev)
        carry_ref[...] = cur[rows - 1:rows, :]
        pm = cur + (prev - cur) * mu_ref[...]

        r = pm[:, 0:wd]
        k = pm[:, wd:2 * wd]
        v = pm[:, 2 * wd:3 * wd]
        lora = pm[:, 3 * wd:3 * wd + LANES]
        xg = pm[:, 3 * wd + LANES:]
        lw = _dot(jnp.tanh(lora).astype(BF16), wlw_ref[...])
        la = _dot(lora.astype(BF16), wla_ref[...])
        g = _dot(jax.nn.sigmoid(xg).astype(BF16), wg_ref[...])
        z = w0_ref[...] + lw
        w = -(jnp.maximum(-z, 0.0) + jnp.log1p(jnp.exp(-jnp.abs(z)))) - 0.5
        ld = -jnp.exp(w)
        a = jax.nn.sigmoid(a0_ref[...] + la)
        kk = k * kk_ref[...]
        kk = kk / jnp.maximum(jnp.sqrt(headsum(kk * kk)), 1e-12)
        k2 = k * (1.0 + (a - 1.0) * ka_ref[...])
        bonus = headsum(r * k2 * rk_ref[...]) * v
        av = -kk
        bv = kk * a

        items = [(r[rs, sl], ld[rs, sl], k2[rs, sl], v[rs, sl], av[rs, sl], bv[rs, sl])
                 for sl in pair_cols for rs in chunk_rows]
        return items, bonus, g

    def state_steps(maps, bonus, g, r0):
        bf = lambda x: x.astype(BF16)
        sts, ys = [None] * npair, [[] for _ in range(npair)]

        def chunk_step(i):
            for j in range(npair):
                rbar, y0, mt, gt = maps[j * chunks_per_step + i]
                s_hi, s_lo = _split(st_ref[j] if i == 0 else sts[j])
                ys[j].append(_dot(bf(rbar), s_hi, NT) + y0)
                mtb = bf(mt)
                sts[j] = (_dot(s_lo, mtb) + _dot(s_hi, mtb)) + gt
                if i == chunks_per_step - 1:
                    st_ref[j] = sts[j]

        def output():
            y = jnp.concatenate([jnp.concatenate(c, axis=0) for c in ys], axis=1)
            mean = headsum(y) * (1.0 / HEAD)
            yc = y - mean
            var = headsum(yc * yc) * (1.0 / HEAD)
            yn = yc * lax.rsqrt(var + RW_LN_EPS) * lng_ref[...] + lnb_ref[...]
            o_ref[r0:r0 + rows, :] = (yn + bonus) * g

        return [partial(chunk_step, i) for i in range(chunks_per_step)] + [output]

    pending = []
    for r0 in range(0, tb, rows):
        items, bonus, g = prepare(r0)
        maps = _chunk_maps(items, consts, hooks=pending)
        pending = state_steps(maps, bonus, g, r0)
    for step in pending:
        step()


def _rwkv(prw, params, batch, seq, tb):
    n_tok, ncol = prw.shape
    wd = params[1].shape[1]
    nt = seq // tb
    const = lambda a: pl.BlockSpec(a.shape, lambda b, t: (0, 0), pipeline_mode=pl.Buffered(1))
    return pl.pallas_call(
        partial(_rwkv_kernel, chunks_per_step=RW_CHUNKS_PER_STEP),
        grid=(batch, nt),
        in_specs=[pl.BlockSpec((tb, ncol), lambda b, t: (b * nt + t, 0))] + [const(a) for a in params],
        out_specs=pl.BlockSpec((tb, wd), lambda b, t: (b * nt + t, 0)),
        out_shape=jax.ShapeDtypeStruct((n_tok, wd), F32),
        scratch_shapes=[pltpu.VMEM((1, ncol), F32), pltpu.VMEM((wd // LANES, LANES, LANES), F32)],
        compiler_params=pltpu.CompilerParams(dimension_semantics=("arbitrary", "arbitrary"),
                                             vmem_limit_bytes=VMEM_LIMIT),
        name="rwkv",
    )(prw, *params)


def _attn_kernel(cur_ref, prev_ref, o_ref, l_ref, *, dil, slopes):
    qb = cur_ref.shape[1]
    npr = o_ref.shape[0]
    span = ATT_L * dil
    n = pl.program_id(1)
    qi = lax.broadcasted_iota(jnp.int32, (ATT_L, 2 * ATT_L), 0)
    kj = lax.broadcasted_iota(jnp.int32, (ATT_L, 2 * ATT_L), 1)
    steps = qi + ATT_L - kj
    band = (steps >= 0) & (steps <= ATT_L)
    first = band & (kj >= jnp.where(n > 0, 0, ATT_L))
    stepsf = steps.astype(F32)
    lane = lax.broadcasted_iota(jnp.int32, (ATT_L, LANES), 1)
    mask0 = lane < HEAD
    neg = jnp.float32(-1e30)
    bias_first = [jnp.where(first, -sl * stepsf, neg) for sl in slopes]
    bias_band = [jnp.where(band, -sl * stepsf, neg) for sl in slopes] if qb > span else None
    for s in range(qb // span):
        for r in range(dil):
            sel = lambda base: pl.ds(base + r, ATT_L, stride=dil) if dil > 1 else pl.ds(base, ATT_L)
            rows = sel(s * span)
            for pr in range(npr):
                if s == 0:
                    kprev, vprev = prev_ref[npr + pr, sel(0), :], prev_ref[2 * npr + pr, sel(0), :]
                    bias = bias_first
                else:
                    prows = sel((s - 1) * span)
                    kprev, vprev = cur_ref[npr + pr, prows, :], cur_ref[2 * npr + pr, prows, :]
                    bias = bias_band
                kp = jnp.concatenate([kprev, cur_ref[npr + pr, rows, :]], axis=0).astype(BF16)
                vp = jnp.concatenate([vprev, cur_ref[2 * npr + pr, rows, :]], axis=0).astype(BF16)
                qp = cur_ref[pr, rows, :]
                o_pair = jnp.zeros((ATT_L, LANES), F32)
                l_pair = jnp.zeros((ATT_L, LANES), F32)
                for hh in range(2):
                    hmask = mask0 if hh == 0 else jnp.logical_not(mask0)
                    qm = jnp.where(hmask, qp, 0.0).astype(BF16)
                    logits = _dot(qm, kp, NT) * (HEAD ** -0.5) + bias[2 * pr + hh]
                    m = jnp.max(logits, axis=1, keepdims=True)
                    e = jnp.exp(logits - m)
                    den = jnp.sum(e, axis=1, keepdims=True)
                    o_h = _dot(e.astype(BF16), vp) / den
                    o_pair = jnp.where(hmask, o_h, o_pair)
                    l_pair = jnp.where(hmask, m + jnp.log(den), l_pair)
                o_ref[pr, rows, :] = o_pair
                l_ref[pr, rows, :] = l_pair


def _attn_group(qkv, batch, seq, dil, slopes):
    nsl, n_tok, _ = qkv.shape
    npr = nsl // 3
    span = ATT_L * dil
    qb = max(span, 512)
    nt = seq // qb
    per = qb // span
    cur = pl.BlockSpec((nsl, qb, LANES), lambda b, n: (0, b * nt + n, 0))
    prev = pl.BlockSpec((nsl, span, LANES), lambda b, n: (0, b * (seq // span) + jnp.maximum(n * per - 1, 0), 0))
    out = pl.BlockSpec((npr, qb, LANES), lambda b, n: (0, b * nt + n, 0))
    return pl.pallas_call(
        partial(_attn_kernel, dil=dil, slopes=tuple(float(s) * dil for s in slopes)),
        grid=(batch, nt),
        in_specs=[cur, prev],
        out_specs=[out, out],
        out_shape=[jax.ShapeDtypeStruct((npr, n_tok, LANES), F32)] * 2,
        compiler_params=pltpu.CompilerParams(dimension_semantics=("arbitrary",) * 2, vmem_limit_bytes=VMEM_LIMIT),
        name=f"attn_d{dil}",
    )(qkv, qkv)


def _merge_kernel(x_ref, ya_ref, o1_ref, l1_ref, o2_ref, l2_ref, o3_ref, l3_ref,
                  gmix_ref, wgate_ref, bgate_ref, wa_ref, wb_ref, wout_ref, out_ref):
    d = x_ref.shape[1]
    x = x_ref[...]
    h = _rms(x, gmix_ref[...]).astype(BF16)
    gates = jax.nn.sigmoid(_dot(h, wgate_ref[...]) + bgate_ref[...])
    cat = lambda ref: jnp.concatenate([ref[c] for c in range(ref.shape[0])], axis=1)
    l1, l2, l3 = cat(l1_ref), cat(l2_ref), cat(l3_ref)
    lm = jnp.maximum(jnp.maximum(l1, l2), l3)
    e1, e2, e3 = jnp.exp(l1 - lm), jnp.exp(l2 - lm), jnp.exp(l3 - lm)
    yb = (e1 * cat(o1_ref) + e2 * cat(o2_ref) + e3 * cat(o3_ref)) / (e1 + e2 + e3)
    merged = (gates[:, :d] * _dot(ya_ref[...].astype(BF16), wa_ref[...])
              + gates[:, d:] * _dot(yb.astype(BF16), wb_ref[...]))
    out_ref[...] = x + _dot(merged.astype(BF16), wout_ref[...])


def _merge(x2, ya, att, weights, tm):
    n_tok, d = x2.shape
    tok = lambda a: pl.BlockSpec((tm, a.shape[1]), lambda i: (i, 0))
    slab = lambda a: pl.BlockSpec((a.shape[0], tm, LANES), lambda i: (0, i, 0))
    const = lambda a: pl.BlockSpec(a.shape, lambda i: (0, 0), pipeline_mode=pl.Buffered(1))
    return pl.pallas_call(
        _merge_kernel,
        grid=(n_tok // tm,),
        in_specs=[tok(x2), tok(ya)] + [slab(a) for a in att] + [const(w) for w in weights],
        out_specs=tok(x2),
        out_shape=jax.ShapeDtypeStruct((n_tok, d), F32),
        compiler_params=pltpu.CompilerParams(dimension_semantics=("arbitrary",), vmem_limit_bytes=VMEM_LIMIT),
        name="merge",
    )(x2, ya, *att, *weights)


def _ffn_kernel(x_ref, p_ref, gffn_ref, wup_ref, cw_ref, cb_ref, wdown_ref, gple_ref, wpg_ref, wple_ref, gfin_ref,
                out_ref, ucarry_ref, *, ff_chunk):
    tm, d = x_ref.shape
    dff = wdown_ref.shape[0]

    @pl.when(pl.program_id(1) == 0)
    def _():
        ucarry_ref[...] = jnp.zeros_like(ucarry_ref)

    x = x_ref[...]
    h = _rms(x, gffn_ref[...]).astype(BF16)
    row8 = lax.broadcasted_iota(jnp.int32, (8, ff_chunk), 0)

    def conv(u, cols):
        tail = ucarry_ref[:, cols]
        ucarry_ref[:, cols] = u[tm - 8:tm, :]
        r1, r2 = pltpu.roll(u, 1, axis=0), pltpu.roll(u, 2, axis=0)
        top1 = jnp.where(row8 == 0, tail[7:8, :], r1[:8])
        top2 = jnp.where(row8 == 0, tail[6:7, :], jnp.where(row8 == 1, tail[7:8, :], r2[:8]))
        s1 = jnp.concatenate([top1, r1[8:]], axis=0)
        s2 = jnp.concatenate([top2, r2[8:]], axis=0)
        return cb_ref[:, cols] + cw_ref[0:1, cols] * u + cw_ref[1:2, cols] * s1 + cw_ref[2:3, cols] * s2

    def up(c0):
        return _dot(h, wup_ref[:, c0:c0 + ff_chunk]), _dot(h, wup_ref[:, dff + c0:dff + c0 + ff_chunk])

    c1 = np.float32(np.sqrt(2.0 / np.pi))
    c3 = np.float32(np.sqrt(2.0 / np.pi) * 0.044715)
    acc = jnp.zeros((tm, d), F32)
    starts = list(range(0, dff, ff_chunk))
    nxt = up(starts[0])
    for i, c0 in enumerate(starts):
        ug, uv = nxt
        if i + 1 < len(starts):
            nxt = up(starts[i + 1])
        gcols = slice(c0, c0 + ff_chunk)
        gate = conv(ug, gcols)
        val = conv(uv, slice(dff + c0, dff + c0 + ff_chunk))
        hv = 0.5 * gate * val
        act = hv + hv * jnp.tanh(gate * (c1 + c3 * (gate * gate)))
        acc = acc + _dot(act.astype(BF16), wdown_ref[gcols, :])
    x = x + acc

    h = _rms(x, gple_ref[...]).astype(BF16)
    pg = jax.nn.sigmoid(_dot(h, wpg_ref[...]))
    x = x + pg * _dot(p_ref[...].astype(BF16), wple_ref[...])
    out_ref[...] = _rms(x, gfin_ref[...])


def _ffn(x2, p2, weights, batch, seq, tm, ff_chunk):
    n_tok, d = x2.shape
    nt = seq // tm
    tok = lambda a: pl.BlockSpec((tm, a.shape[1]), lambda b, t: (b * nt + t, 0))
    const = lambda a: pl.BlockSpec(a.shape, lambda b, t: (0, 0), pipeline_mode=pl.Buffered(1))
    n_up = weights[1].shape[1]
    return pl.pallas_call(
        partial(_ffn_kernel, ff_chunk=ff_chunk),
        grid=(batch, nt),
        in_specs=[tok(x2), tok(p2)] + [const(w) for w in weights],
        out_specs=tok(x2),
        out_shape=jax.ShapeDtypeStruct((n_tok, d), F32),
        scratch_shapes=[pltpu.VMEM((8, n_up), F32)],
        compiler_params=pltpu.CompilerParams(dimension_semantics=("arbitrary", "arbitrary"),
                                             vmem_limit_bytes=VMEM_LIMIT),
        name="ffn",
    )(x2, p2, *weights)


def _layer(x2, p2, batch, seq, g_mix, w_in, rw_mu, rw_w0, rw_w_up, rw_a0, rw_a_up, rw_g_up, rw_k_k, rw_k_a, rw_r_k,
           rw_ln_g, rw_ln_b, w_branch_a, w_branch_b, w_gate, b_gate, w_out, g_ffn, w_up, conv_w, conv_b,
           w_down, g_ple, w_ple_gate, w_ple, g_out):
    d = x2.shape[1]
    rw_w = rw_w0.shape[0]
    n_dec, n_aaa, n_gate = rw_w_up.shape[0], rw_a_up.shape[0], rw_g_up.shape[0]
    rw_cols = 3 * rw_w + n_dec + n_aaa + n_gate
    att_w = (w_in.shape[1] - rw_cols) // 3
    grp_w = att_w // len(ATT_GROUPS)
    assert n_dec + n_aaa == LANES and n_gate <= 2 * LANES
    row = lambda a: a.reshape(1, -1)

    gpad = 2 * LANES - n_gate
    wrw = jnp.pad(w_in[:, :rw_cols], ((0, 0), (0, gpad))).astype(BF16)
    mu = jnp.pad(rw_mu, (0, gpad)).reshape(1, -1)
    watt = []
    for gi in range(len(ATT_GROUPS)):
        cols = [w_in[:, rw_cols + part * att_w + gi * grp_w: rw_cols + part * att_w + (gi + 1) * grp_w] for part in range(3)]
        watt.append(jnp.concatenate(cols, axis=1).astype(BF16))
    wlw = jnp.pad(rw_w_up, ((0, n_aaa), (0, 0))).astype(BF16)
    wla = jnp.pad(rw_a_up, ((n_dec, 0), (0, 0))).astype(BF16)
    wg = jnp.pad(rw_g_up, ((0, gpad), (0, 0))).astype(BF16)

    prw, q1, q2, q3 = _in_proj(x2, row(g_mix), wrw, watt, tm=TOKEN_TILE)
    rw_params = (mu, row(rw_w0), wlw, wla, row(rw_a0), wg, row(rw_k_k), row(rw_k_a), row(rw_r_k),
                 row(rw_ln_g), row(rw_ln_b))
    ya = _rwkv(prw, rw_params, batch, seq, tb=RW_TOKEN_TILE)

    n_heads = len(ATT_GROUPS) * grp_w // HEAD
    slopes = [2.0 ** (-8.0 * (h + 1) / n_heads) for h in range(n_heads)]
    att = []
    for gi, (qkv, (window, dil)) in enumerate(zip((q1, q2, q3), ATT_GROUPS)):
        assert window // dil == ATT_L
        hpg = grp_w // HEAD
        att.extend(_attn_group(qkv, batch, seq, dil, slopes[gi * hpg:(gi + 1) * hpg]))

    merge_w = (row(g_mix), w_gate.astype(BF16), row(b_gate), w_branch_a.astype(BF16), w_branch_b.astype(BF16),
               w_out.astype(BF16))
    x2 = _merge(x2, ya, att, merge_w, tm=TOKEN_TILE)
    ffn_w = (row(g_ffn), w_up.astype(BF16), conv_w, row(conv_b), w_down.astype(BF16),
             row(g_ple), w_ple_gate.astype(BF16), w_ple.astype(BF16), row(g_out))
    return _ffn(x2, p2, ffn_w, batch, seq, tm=TOKEN_TILE, ff_chunk=FF_CHUNK)


def kernel(x, p, g_mix, w_in, rw_mu, rw_w0, rw_w_up, rw_a0, rw_a_up, rw_g_up, rw_k_k, rw_k_a, rw_r_k, rw_ln_g, rw_ln_b, w_branch_a, w_branch_b, w_gate, b_gate, w_out, g_ffn, w_up, conv_w, conv_b, w_down, g_ple, w_ple_gate, w_ple, g_final):
    batch, seq, d = x.shape
    depth = w_in.shape[0]
    x2 = x.reshape(batch * seq, d)
    per_layer = (g_mix, w_in, rw_mu, rw_w0, rw_w_up, rw_a0, rw_a_up, rw_g_up, rw_k_k, rw_k_a, rw_r_k, rw_ln_g,
                 rw_ln_b, w_branch_a, w_branch_b, w_gate, b_gate, w_out, g_ffn, w_up, conv_w, conv_b, w_down,
                 g_ple, w_ple_gate, w_ple)
    for i in range(depth):
        assert depth == 1
        x2 = _layer(x2, p[i].reshape(batch * seq, -1), batch, seq, *[w[i] for w in per_layer], g_final)
    return x2.reshape(batch, seq, d)
```

```python
from functools import partial

import numpy as np
import jax
import jax.numpy as jnp
from jax import lax
from jax.experimental import pallas as pl
from jax.experimental.pallas import tpu as pltpu

F32 = jnp.float32
BF16 = jnp.bfloat16

NORM_EPS = 1e-6
RW_LN_EPS = 64e-5
HEAD = 64
LANES = 128
CHUNK = 64
ATT_L = 128
ATT_GROUPS = ((128, 1), (512, 4), (2048, 16))
VMEM_LIMIT = 56 * 1024 * 1024
TOKEN_TILE = 512
FF_CHUNK = 512
FF_AHEAD = 2
RW_CHUNKS_PER_STEP = 4
RW_TOKEN_TILE = 512

NN = (((1,), (0,)), ((), ()))
NT = (((1,), (1,)), ((), ()))


def _dot(a, b, dims=NN):
    return lax.dot_general(a, b, dims, preferred_element_type=F32)


def _split(x):
    hi = x.astype(BF16)
    lo = (x - hi.astype(F32)).astype(BF16)
    return hi, lo


def _rms(x, g):
    return x * lax.rsqrt(jnp.mean(x * x, axis=-1, keepdims=True) + NORM_EPS) * g


def _in_proj_kernel(x_ref, g_ref, wrw_ref, w1_ref, w2_ref, w3_ref, prw_ref, q1_ref, q2_ref, q3_ref):
    h = _rms(x_ref[...], g_ref[...]).astype(BF16)
    prw_ref[...] = _dot(h, wrw_ref[...])
    for w_ref, o_ref in ((w1_ref, q1_ref), (w2_ref, q2_ref), (w3_ref, q3_ref)):
        qkv = _dot(h, w_ref[...])
        for c in range(o_ref.shape[0]):
            o_ref[c] = qkv[:, c * LANES:(c + 1) * LANES]


def _in_proj(x2, g, wrw, watt, tm):
    n_tok, d = x2.shape
    const = lambda shape: pl.BlockSpec(shape, lambda i: (0, 0), pipeline_mode=pl.Buffered(1))
    tok = lambda n: pl.BlockSpec((tm, n), lambda i: (i, 0))
    slabs = [w.shape[1] // LANES for w in watt]
    return pl.pallas_call(
        _in_proj_kernel,
        grid=(n_tok // tm,),
        in_specs=[tok(d), const(g.shape), const(wrw.shape)] + [const(w.shape) for w in watt],
        out_specs=[tok(wrw.shape[1])] + [pl.BlockSpec((n, tm, LANES), lambda i: (0, i, 0)) for n in slabs],
        out_shape=[jax.ShapeDtypeStruct((n_tok, wrw.shape[1]), F32)]
        + [jax.ShapeDtypeStruct((n, n_tok, LANES), F32) for n in slabs],
        compiler_params=pltpu.CompilerParams(dimension_semantics=("arbitrary",), vmem_limit_bytes=VMEM_LIMIT),
        name="in_proj",
    )(x2, g, wrw, *watt)


def _stack(x, mask0):
    return jnp.concatenate([jnp.where(mask0, x, 0.0), jnp.where(mask0, 0.0, x)], axis=0)


def _chunk_maps(items, consts, hooks=()):
    ltri, mask0, strict, incl, eye = consts
    bf = lambda x: x.astype(BF16)
    each = lambda f, *cols: [f(*xs) for xs in zip(*cols)]
    stack = lambda x: _stack(x, mask0)
    r, ld, k, v, av, bv = (list(c) for c in zip(*items))
    hooks = list(hooks)
    run_hook = lambda: hooks.pop(0)() if hooks else None

    h1 = each(bf, ld)
    rem = each(lambda x, h: x - h.astype(F32), ld, h1)
    h2 = each(bf, rem)
    h3 = each(lambda x, h: bf(x - h.astype(F32)), rem, h2)
    cum = each(lambda a, b, c: (_dot(ltri, c) + _dot(ltri, b)) + _dot(ltri, a), h1, h2, h3)
    tot = each(lambda c: c[CHUNK - 1:CHUNK, :], cum)
    run_hook()
    at = each(lambda a, c, l: a * jnp.exp(c - l), av, cum, ld)
    rt = each(lambda x, c: x * jnp.exp(c), r, cum)
    w_inv = each(lambda c: jnp.exp(-c), cum)
    w_end = each(lambda t, c: jnp.exp(t - c), tot, cum)
    run_hook()
    kst = each(lambda x, w: bf(stack(x * w)), k, w_inv)
    bst = each(lambda x, w: bf(stack(x * w)), bv, w_inv)
    vst_f = each(stack, v)
    vst = each(bf, vst_f)
    vst_t = each(lambda x: bf(x.T), vst_f)
    run_hook()
    kwst = each(lambda x, w: bf(stack(x * w)), k, w_end)
    bwst = each(lambda x, w: bf(stack(x * w)), bv, w_end)

    lhs = each(lambda a, b: bf(jnp.concatenate([a, b], axis=0)), at, rt)
    res_b = each(lambda a, b: _dot(a, b, NT), lhs, bst)
    res_k = each(lambda a, b: _dot(a, b, NT), lhs, kst)
    run_hook()
    a_ab = each(lambda x: jnp.where(strict, x[:CHUNK], 0.0), res_b)
    a_rb = each(lambda x: bf(jnp.where(incl, x[CHUNK:], 0.0)), res_b)
    a_ak = each(lambda x: bf(jnp.where(strict, x[:CHUNK], 0.0)), res_k)
    a_rk = each(lambda x: bf(jnp.where(incl, x[CHUNK:], 0.0)), res_k)

    n_bd = each(stack, a_ab)
    tinv = each(lambda x: eye + x, n_bd)
    pw = each(bf, n_bd)
    while hooks:
        run_hook()
    for _ in range(5):
        pw = each(lambda x: bf(_dot(x, x)), pw)
        tinv = each(lambda t, p: t + _dot(bf(t), p), tinv, pw)

    x0 = each(_dot, a_ak, vst)
    rhs = each(lambda a, x: bf(jnp.concatenate([stack(a), stack(x)], axis=1)), at, x0)
    zf = each(lambda t, x: _dot(bf(t), x), tinv, rhs)
    z = each(bf, zf)
    z_t = each(lambda x: bf(jnp.concatenate([x[:, :LANES].T, x[:, LANES:].T], axis=0)), zf)
    ry = each(_dot, a_rb, z)
    mg = each(_dot, z_t, bwst)
    rbar = each(lambda a, b: a + b[:, :LANES], rt, ry)
    y0 = each(lambda a, x, b: _dot(a, x) + b[:, LANES:], a_rk, vst, ry)
    mt = each(lambda t, x: jnp.where(eye > 0, jnp.exp(t), 0.0) + x[:LANES], tot, mg)
    gt = each(lambda a, x, b: _dot(a, x) + b[LANES:], vst_t, kwst, mg)
    return list(zip(rbar, y0, mt, gt))


def _rwkv_kernel(p_ref, mu_ref, w0_ref, wlw_ref, wla_ref, a0_ref, wg_ref, kk_ref, ka_ref, rk_ref,
                 lng_ref, lnb_ref, o_ref, carry_ref, st_ref, *, chunks_per_step):
    tb = p_ref.shape[0]
    wd = o_ref.shape[1]
    npair = wd // LANES
    rows = chunks_per_step * CHUNK

    @pl.when(pl.program_id(1) == 0)
    def _():
        carry_ref[...] = jnp.zeros_like(carry_ref)
        st_ref[...] = jnp.zeros_like(st_ref)

    row = lax.broadcasted_iota(jnp.int32, (CHUNK, LANES), 0)
    lane = lax.broadcasted_iota(jnp.int32, (CHUNK, LANES), 1)
    mask0 = lane < HEAD
    src = lane & (HEAD - 1)
    r2 = lax.broadcasted_iota(jnp.int32, (2 * CHUNK, LANES), 0)
    l2 = lax.broadcasted_iota(jnp.int32, (2 * CHUNK, LANES), 1)
    eye = (r2 == l2).astype(F32)
    lr = lax.broadcasted_iota(jnp.int32, (CHUNK, CHUNK), 0)
    lc = lax.broadcasted_iota(jnp.int32, (CHUNK, CHUNK), 1)
    ltri = (lc <= lr).astype(BF16)
    consts = (ltri, mask0, src < row, src <= row, eye)
    e_pair = ((r2 < HEAD) == (l2 < HEAD)).astype(BF16)

    def headsum(x):
        return jnp.concatenate(
            [_dot(x[:, j * LANES:(j + 1) * LANES].astype(BF16), e_pair) for j in range(npair)], axis=1)

    pair_cols = [slice(j * LANES, (j + 1) * LANES) for j in range(npair)]
    chunk_rows = [slice(i * CHUNK, (i + 1) * CHUNK) for i in range(chunks_per_step)]

    def prepare(r0):
        cur = p_ref[r0:r0 + rows, :]
        prev = pltpu.roll(cur, 1, axis=0)
        prow = lax.broadcasted_iota(jnp.int32, (8, cur.shape[1]), 0)
        prev = jnp.concatenate([jnp.where(prow == 0, carry_ref[...], prev[:8]), prev[8:]], axis=0)
        carry_ref[...] = cur[rows - 1:rows, :]
        pm = cur + (prev - cur) * mu_ref[...]

        r = pm[:, 0:wd]
        k = pm[:, wd:2 * wd]
        v = pm[:, 2 * wd:3 * wd]
        lora = pm[:, 3 * wd:3 * wd + LANES]
        xg = pm[:, 3 * wd + LANES:]
        lw = _dot(jnp.tanh(lora).astype(BF16), wlw_ref[...])
        la = _dot(lora.astype(BF16), wla_ref[...])
        g = _dot(jax.nn.sigmoid(xg).astype(BF16), wg_ref[...])
        z = w0_ref[...] + lw
        w = -(jnp.maximum(-z, 0.0) + jnp.log1p(jnp.exp(-jnp.abs(z)))) - 0.5
        ld = -jnp.exp(w)
        a = jax.nn.sigmoid(a0_ref[...] + la)
        kk = k * kk_ref[...]
        kk = kk / jnp.maximum(jnp.sqrt(headsum(kk * kk)), 1e-12)
        k2 = k * (1.0 + (a - 1.0) * ka_ref[...])
        bonus = headsum(r * k2 * rk_ref[...]) * v
        av = -kk
        bv = kk * a

        items = [(r[rs, sl], ld[rs, sl], k2[rs, sl], v[rs, sl], av[rs, sl], bv[rs, sl])
                 for sl in pair_cols for rs in chunk_rows]
        return items, bonus, g

    def state_steps(maps, bonus, g, r0):
        bf = lambda x: x.astype(BF16)
        sts, ys = [None] * npair, [[] for _ in range(npair)]

        def chunk_step(i):
            for j in range(npair):
                rbar, y0, mt, gt = maps[j * chunks_per_step + i]
                s_hi, s_lo = _split(st_ref[j] if i == 0 else sts[j])
                ys[j].append(_dot(bf(rbar), s_hi, NT) + y0)
                mtb = bf(mt)
                sts[j] = (_dot(s_lo, mtb) + _dot(s_hi, mtb)) + gt
                if i == chunks_per_step - 1:
                    st_ref[j] = sts[j]

        def output():
            y = jnp.concatenate([jnp.concatenate(c, axis=0) for c in ys], axis=1)
            mean = headsum(y) * (1.0 / HEAD)
            yc = y - mean
            var = headsum(yc * yc) * (1.0 / HEAD)
            yn = yc * lax.rsqrt(var + RW_LN_EPS) * lng_ref[...] + lnb_ref[...]
            o_ref[r0:r0 + rows, :] = (yn + bonus) * g

        return [partial(chunk_step, i) for i in range(chunks_per_step)] + [output]

    pending = []
    for r0 in range(0, tb, rows):
        items, bonus, g = prepare(r0)
        maps = _chunk_maps(items, consts, hooks=pending)
        pending = state_steps(maps, bonus, g, r0)
    for step in pending:
        step()


def _rwkv(prw, params, batch, seq, tb):
    n_tok, ncol = prw.shape
    wd = params[1].shape[1]
    nt = seq // tb
    const = lambda a: pl.BlockSpec(a.shape, lambda b, t: (0, 0), pipeline_mode=pl.Buffered(1))
    return pl.pallas_call(
        partial(_rwkv_kernel, chunks_per_step=RW_CHUNKS_PER_STEP),
        grid=(batch, nt),
        in_specs=[pl.BlockSpec((tb, ncol), lambda b, t: (b * nt + t, 0))] + [const(a) for a in params],
        out_specs=pl.BlockSpec((tb, wd), lambda b, t: (b * nt + t, 0)),
        out_shape=jax.ShapeDtypeStruct((n_tok, wd), F32),
        scratch_shapes=[pltpu.VMEM((1, ncol), F32), pltpu.VMEM((wd // LANES, LANES, LANES), F32)],
        compiler_params=pltpu.CompilerParams(dimension_semantics=("arbitrary", "arbitrary"),
                                             vmem_limit_bytes=VMEM_LIMIT),
        name="rwkv",
    )(prw, *params)


def _attn_kernel(cur_ref, prev_ref, o_ref, l_ref, *, dil, slopes):
    qb = cur_ref.shape[1]
    npr = o_ref.shape[0]
    span = ATT_L * dil
    n = pl.program_id(1)
    qi = lax.broadcasted_iota(jnp.int32, (ATT_L, 2 * ATT_L), 0)
    kj = lax.broadcasted_iota(jnp.int32, (ATT_L, 2 * ATT_L), 1)
    steps = qi + ATT_L - kj
    band = (steps >= 0) & (steps <= ATT_L)
    first = band & (kj >= jnp.where(n > 0, 0, ATT_L))
    stepsf = steps.astype(F32)
    lane = lax.broadcasted_iota(jnp.int32, (ATT_L, LANES), 1)
    mask0 = lane < HEAD
    neg = jnp.float32(-1e30)
    bias_first = [jnp.where(first, -sl * stepsf, neg) for sl in slopes]
    bias_band = [jnp.where(band, -sl * stepsf, neg) for sl in slopes] if qb > span else None
    for s in range(qb // span):
        for r in range(dil):
            sel = lambda base: pl.ds(base + r, ATT_L, stride=dil) if dil > 1 else pl.ds(base, ATT_L)
            rows = sel(s * span)
            for pr in range(npr):
                if s == 0:
                    kprev, vprev = prev_ref[npr + pr, sel(0), :], prev_ref[2 * npr + pr, sel(0), :]
                    bias = bias_first
                else:
                    prows = sel((s - 1) * span)
                    kprev, vprev = cur_ref[npr + pr, prows, :], cur_ref[2 * npr + pr, prows, :]
                    bias = bias_band
                kp = jnp.concatenate([kprev, cur_ref[npr + pr, rows, :]], axis=0).astype(BF16)
                vp = jnp.concatenate([vprev, cur_ref[2 * npr + pr, rows, :]], axis=0).astype(BF16)
                qp = cur_ref[pr, rows, :]
                o_pair = jnp.zeros((ATT_L, LANES), F32)
                l_pair = jnp.zeros((ATT_L, LANES), F32)
                for hh in range(2):
                    hmask = mask0 if hh == 0 else jnp.logical_not(mask0)
                    qm = jnp.where(hmask, qp, 0.0).astype(BF16)
                    logits = _dot(qm, kp, NT) * (HEAD ** -0.5) + bias[2 * pr + hh]
                    m = jnp.max(logits, axis=1, keepdims=True)
                    e = jnp.exp(logits - m)
                    den = jnp.sum(e, axis=1, keepdims=True)
                    o_h = _dot(e.astype(BF16), vp) / den
                    o_pair = jnp.where(hmask, o_h, o_pair)
                    l_pair = jnp.where(hmask, m + jnp.log(den), l_pair)
                o_ref[pr, rows, :] = o_pair
                l_ref[pr, rows, :] = l_pair


def _attn_group(qkv, batch, seq, dil, slopes):
    nsl, n_tok, _ = qkv.shape
    npr = nsl // 3
    span = ATT_L * dil
    qb = max(span, 512)
    nt = seq // qb
    per = qb // span
    cur = pl.BlockSpec((nsl, qb, LANES), lambda b, n: (0, b * nt + n, 0))
    prev = pl.BlockSpec((nsl, span, LANES), lambda b, n: (0, b * (seq // span) + jnp.maximum(n * per - 1, 0), 0))
    out = pl.BlockSpec((npr, qb, LANES), lambda b, n: (0, b * nt + n, 0))
    return pl.pallas_call(
        partial(_attn_kernel, dil=dil, slopes=tuple(float(s) * dil for s in slopes)),
        grid=(batch, nt),
        in_specs=[cur, prev],
        out_specs=[out, out],
        out_shape=[jax.ShapeDtypeStruct((npr, n_tok, LANES), F32)] * 2,
        compiler_params=pltpu.CompilerParams(dimension_semantics=("arbitrary",) * 2, vmem_limit_bytes=VMEM_LIMIT),
        name=f"attn_d{dil}",
    )(qkv, qkv)


def _merge_kernel(x_ref, ya_ref, o1_ref, l1_ref, o2_ref, l2_ref, o3_ref, l3_ref,
                  gmix_ref, wgate_ref, bgate_ref, wa_ref, wb_ref, wout_ref, out_ref):
    d = x_ref.shape[1]
    x = x_ref[...]
    h = _rms(x, gmix_ref[...]).astype(BF16)
    gates = jax.nn.sigmoid(_dot(h, wgate_ref[...]) + bgate_ref[...])
    cat = lambda ref: jnp.concatenate([ref[c] for c in range(ref.shape[0])], axis=1)
    l1, l2, l3 = cat(l1_ref), cat(l2_ref), cat(l3_ref)
    lm = jnp.maximum(jnp.maximum(l1, l2), l3)
    e1, e2, e3 = jnp.exp(l1 - lm), jnp.exp(l2 - lm), jnp.exp(l3 - lm)
    yb = (e1 * cat(o1_ref) + e2 * cat(o2_ref) + e3 * cat(o3_ref)) / (e1 + e2 + e3)
    merged = (gates[:, :d] * _dot(ya_ref[...].astype(BF16), wa_ref[...])
              + gates[:, d:] * _dot(yb.astype(BF16), wb_ref[...]))
    out_ref[...] = x + _dot(merged.astype(BF16), wout_ref[...])


def _merge(x2, ya, att, weights, tm):
    n_tok, d = x2.shape
    tok = lambda a: pl.BlockSpec((tm, a.shape[1]), lambda i: (i, 0))
    slab = lambda a: pl.BlockSpec((a.shape[0], tm, LANES), lambda i: (0, i, 0))
    const = lambda a: pl.BlockSpec(a.shape, lambda i: (0, 0), pipeline_mode=pl.Buffered(1))
    return pl.pallas_call(
        _merge_kernel,
        grid=(n_tok // tm,),
        in_specs=[tok(x2), tok(ya)] + [slab(a) for a in att] + [const(w) for w in weights],
        out_specs=tok(x2),
        out_shape=jax.ShapeDtypeStruct((n_tok, d), F32),
        compiler_params=pltpu.CompilerParams(dimension_semantics=("arbitrary",), vmem_limit_bytes=VMEM_LIMIT),
        name="merge",
    )(x2, ya, *att, *weights)


def _ffn_kernel(x_ref, p_ref, gffn_ref, wup_ref, cw_ref, cb_ref, wdown_ref, gple_ref, wpg_ref, wple_ref, gfin_ref,
                out_ref, ucarry_ref, *, ff_chunk):
    tm, d = x_ref.shape
    dff = wdown_ref.shape[0]

    @pl.when(pl.program_id(1) == 0)
    def _():
        ucarry_ref[...] = jnp.zeros_like(ucarry_ref)

    x = x_ref[...]
    h = _rms(x, gffn_ref[...]).astype(BF16)
    row8 = lax.broadcasted_iota(jnp.int32, (8, ff_chunk), 0)

    def conv(u, cols):
        tail = ucarry_ref[:, cols]
        ucarry_ref[:, cols] = u[tm - 8:tm, :]
        r1, r2 = pltpu.roll(u, 1, axis=0), pltpu.roll(u, 2, axis=0)
        top1 = jnp.where(row8 == 0, tail[7:8, :], r1[:8])
        top2 = jnp.where(row8 == 0, tail[6:7, :], jnp.where(row8 == 1, tail[7:8, :], r2[:8]))
        s1 = jnp.concatenate([top1, r1[8:]], axis=0)
        s2 = jnp.concatenate([top2, r2[8:]], axis=0)
        return cb_ref[:, cols] + cw_ref[0:1, cols] * u + cw_ref[1:2, cols] * s1 + cw_ref[2:3, cols] * s2

    def up(c0):
        return _dot(h, wup_ref[:, c0:c0 + ff_chunk]), _dot(h, wup_ref[:, dff + c0:dff + c0 + ff_chunk])

    c1 = np.float32(np.sqrt(2.0 / np.pi))
    c3 = np.float32(np.sqrt(2.0 / np.pi) * 0.044715)
    acc = jnp.zeros((tm, d), F32)
    starts = list(range(0, dff, ff_chunk))
    ahead = [up(c0) for c0 in starts[:FF_AHEAD]]
    for i, c0 in enumerate(starts):
        ug, uv = ahead.pop(0)
        if i + FF_AHEAD < len(starts):
            ahead.append(up(starts[i + FF_AHEAD]))
        gcols = slice(c0, c0 + ff_chunk)
        gate = conv(ug, gcols)
        val = conv(uv, slice(dff + c0, dff + c0 + ff_chunk))
        hv = 0.5 * gate * val
        act = hv + hv * jnp.tanh(gate * (c1 + c3 * (gate * gate)))
        acc = acc + _dot(act.astype(BF16), wdown_ref[gcols, :])
    x = x + acc

    h = _rms(x, gple_ref[...]).astype(BF16)
    pg = jax.nn.sigmoid(_dot(h, wpg_ref[...]))
    x = x + pg * _dot(p_ref[...].astype(BF16), wple_ref[...])
    out_ref[...] = _rms(x, gfin_ref[...])


def _ffn(x2, p2, weights, batch, seq, tm, ff_chunk):
    n_tok, d = x2.shape
    nt = seq // tm
    tok = lambda a: pl.BlockSpec((tm, a.shape[1]), lambda b, t: (b * nt + t, 0))
    const = lambda a: pl.BlockSpec(a.shape, lambda b, t: (0, 0), pipeline_mode=pl.Buffered(1))
    n_up = weights[1].shape[1]
    return pl.pallas_call(
        partial(_ffn_kernel, ff_chunk=ff_chunk),
        grid=(batch, nt),
        in_specs=[tok(x2), tok(p2)] + [const(w) for w in weights],
        out_specs=tok(x2),
        out_shape=jax.ShapeDtypeStruct((n_tok, d), F32),
        scratch_shapes=[pltpu.VMEM((8, n_up), F32)],
        compiler_params=pltpu.CompilerParams(dimension_semantics=("arbitrary", "arbitrary"),
                                             vmem_limit_bytes=VMEM_LIMIT),
        name="ffn",
    )(x2, p2, *weights)


def _layer(x2, p2, batch, seq, g_mix, w_in, rw_mu, rw_w0, rw_w_up, rw_a0, rw_a_up, rw_g_up, rw_k_k, rw_k_a, rw_r_k,
           rw_ln_g, rw_ln_b, w_branch_a, w_branch_b, w_gate, b_gate, w_out, g_ffn, w_up, conv_w, conv_b,
           w_down, g_ple, w_ple_gate, w_ple, g_out):
    d = x2.shape[1]
    rw_w = rw_w0.shape[0]
    n_dec, n_aaa, n_gate = rw_w_up.shape[0], rw_a_up.shape[0], rw_g_up.shape[0]
    rw_cols = 3 * rw_w + n_dec + n_aaa + n_gate
    att_w = (w_in.shape[1] - rw_cols) // 3
    grp_w = att_w // len(ATT_GROUPS)
    assert n_dec + n_aaa == LANES and n_gate <= 2 * LANES
    row = lambda a: a.reshape(1, -1)

    gpad = 2 * LANES - n_gate
    wrw = jnp.pad(w_in[:, :rw_cols], ((0, 0), (0, gpad))).astype(BF16)
    mu = jnp.pad(rw_mu, (0, gpad)).reshape(1, -1)
    watt = []
    for gi in range(len(ATT_GROUPS)):
        cols = [w_in[:, rw_cols + part * att_w + gi * grp_w: rw_cols + part * att_w + (gi + 1) * grp_w] for part in range(3)]
        watt.append(jnp.concatenate(cols, axis=1).astype(BF16))
    wlw = jnp.pad(rw_w_up, ((0, n_aaa), (0, 0))).astype(BF16)
    wla = jnp.pad(rw_a_up, ((n_dec, 0), (0, 0))).astype(BF16)
    wg = jnp.pad(rw_g_up, ((0, gpad), (0, 0))).astype(BF16)

    prw, q1, q2, q3 = _in_proj(x2, row(g_mix), wrw, watt, tm=TOKEN_TILE)
    rw_params = (mu, row(rw_w0), wlw, wla, row(rw_a0), wg, row(rw_k_k), row(rw_k_a), row(rw_r_k),
                 row(rw_ln_g), row(rw_ln_b))
    ya = _rwkv(prw, rw_params, batch, seq, tb=RW_TOKEN_TILE)

    n_heads = len(ATT_GROUPS) * grp_w // HEAD
    slopes = [2.0 ** (-8.0 * (h + 1) / n_heads) for h in range(n_heads)]
    att = []
    for gi, (qkv, (window, dil)) in enumerate(zip((q1, q2, q3), ATT_GROUPS)):
        assert window // dil == ATT_L
        hpg = grp_w // HEAD
        att.extend(_attn_group(qkv, batch, seq, dil, slopes[gi * hpg:(gi + 1) * hpg]))

    merge_w = (row(g_mix), w_gate.astype(BF16), row(b_gate), w_branch_a.astype(BF16), w_branch_b.astype(BF16),
               w_out.astype(BF16))
    x2 = _merge(x2, ya, att, merge_w, tm=TOKEN_TILE)
    ffn_w = (row(g_ffn), w_up.astype(BF16), conv_w, row(conv_b), w_down.astype(BF16),
             row(g_ple), w_ple_gate.astype(BF16), w_ple.astype(BF16), row(g_out))
    return _ffn(x2, p2, ffn_w, batch, seq, tm=TOKEN_TILE, ff_chunk=FF_CHUNK)


def kernel(x, p, g_mix, w_in, rw_mu, rw_w0, rw_w_up, rw_a0, rw_a_up, rw_g_up, rw_k_k, rw_k_a, rw_r_k, rw_ln_g, rw_ln_b, w_branch_a, w_branch_b, w_gate, b_gate, w_out, g_ffn, w_up, conv_w, conv_b, w_down, g_ple, w_ple_gate, w_ple, g_final):
    batch, seq, d = x.shape
    depth = w_in.shape[0]
    x2 = x.reshape(batch * seq, d)
    per_layer = (g_mix, w_in, rw_mu, rw_w0, rw_w_up, rw_a0, rw_a_up, rw_g_up, rw_k_k, rw_k_a, rw_r_k, rw_ln_g,
                 rw_ln_b, w_branch_a, w_branch_b, w_gate, b_gate, w_out, g_ffn, w_up, conv_w, conv_b, w_down,
                 g_ple, w_ple_gate, w_ple)
    for i in range(depth):
        assert depth == 1
        x2 = _layer(x2, p[i].reshape(batch * seq, -1), batch, seq, *[w[i] for w in per_layer], g_final)
    return x2.reshape(batch, seq, d)
```

```python
from functools import partial

import numpy as np
import jax
import jax.numpy as jnp
from jax import lax
from jax.experimental import pallas as pl
from jax.experimental.pallas import tpu as pltpu

F32 = jnp.float32
BF16 = jnp.bfloat16

NORM_EPS = 1e-6
RW_LN_EPS = 64e-5
HEAD = 64
LANES = 128
CHUNK = 64
ATT_L = 128
ATT_GROUPS = ((128, 1), (512, 4), (2048, 16))
VMEM_LIMIT = 56 * 1024 * 1024
TOKEN_TILE = 512
FF_CHUNK = 512
FF_AHEAD = 2
RW_CHUNKS_PER_STEP = 4
RW_TOKEN_TILE = 512

NN = (((1,), (0,)), ((), ()))
NT = (((1,), (1,)), ((), ()))


def _dot(a, b, dims=NN):
    return lax.dot_general(a, b, dims, preferred_element_type=F32)


def _split(x):
    hi = x.astype(BF16)
    lo = (x - hi.astype(F32)).astype(BF16)
    return hi, lo


def _rms(x, g):
    return x * lax.rsqrt(jnp.mean(x * x, axis=-1, keepdims=True) + NORM_EPS) * g


def _in_proj_kernel(x_ref, g_ref, wrw_ref, w1_ref, w2_ref, w3_ref, prw_ref, q1_ref, q2_ref, q3_ref):
    h = _rms(x_ref[...], g_ref[...]).astype(BF16)
    prw_ref[...] = _dot(h, wrw_ref[...])
    for w_ref, o_ref in ((w1_ref, q1_ref), (w2_ref, q2_ref), (w3_ref, q3_ref)):
        qkv = _dot(h, w_ref[...])
        for c in range(o_ref.shape[0]):
            o_ref[c] = qkv[:, c * LANES:(c + 1) * LANES]


def _in_proj(x2, g, wrw, watt, tm):
    n_tok, d = x2.shape
    const = lambda shape: pl.BlockSpec(shape, lambda i: (0, 0), pipeline_mode=pl.Buffered(1))
    tok = lambda n: pl.BlockSpec((tm, n), lambda i: (i, 0))
    slabs = [w.shape[1] // LANES for w in watt]
    return pl.pallas_call(
        _in_proj_kernel,
        grid=(n_tok // tm,),
        in_specs=[tok(d), const(g.shape), const(wrw.shape)] + [const(w.shape) for w in watt],
        out_specs=[tok(wrw.shape[1])] + [pl.BlockSpec((n, tm, LANES), lambda i: (0, i, 0)) for n in slabs],
        out_shape=[jax.ShapeDtypeStruct((n_tok, wrw.shape[1]), F32)]
        + [jax.ShapeDtypeStruct((n, n_tok, LANES), F32) for n in slabs],
        compiler_params=pltpu.CompilerParams(dimension_semantics=("arbitrary",), vmem_limit_bytes=VMEM_LIMIT),
        name="in_proj",
    )(x2, g, wrw, *watt)


def _stack(x, mask0):
    return jnp.concatenate([jnp.where(mask0, x, 0.0), jnp.where(mask0, 0.0, x)], axis=0)


def _chunk_maps(items, consts, hooks=()):
    ltri, mask0, strict, incl, eye = consts
    bf = lambda x: x.astype(BF16)
    each = lambda f, *cols: [f(*xs) for xs in zip(*cols)]
    stack = lambda x: _stack(x, mask0)
    r, ld, k, v, av, bv = (list(c) for c in zip(*items))
    hooks = list(hooks)
    run_hook = lambda: hooks.pop(0)() if hooks else None

    h1 = each(bf, ld)
    rem = each(lambda x, h: x - h.astype(F32), ld, h1)
    h2 = each(bf, rem)
    h3 = each(lambda x, h: bf(x - h.astype(F32)), rem, h2)
    cum = each(lambda a, b, c: (_dot(ltri, c) + _dot(ltri, b)) + _dot(ltri, a), h1, h2, h3)
    tot = each(lambda c: c[CHUNK - 1:CHUNK, :], cum)
    run_hook()
    at = each(lambda a, c, l: a * jnp.exp(c - l), av, cum, ld)
    rt = each(lambda x, c: x * jnp.exp(c), r, cum)
    w_inv = each(lambda c: jnp.exp(-c), cum)
    w_end = each(lambda t, c: jnp.exp(t - c), tot, cum)
    run_hook()
    kst = each(lambda x, w: bf(stack(x * w)), k, w_inv)
    bst = each(lambda x, w: bf(stack(x * w)), bv, w_inv)
    vst_f = each(stack, v)
    vst = each(bf, vst_f)
    vst_t = each(lambda x: bf(x.T), vst_f)
    run_hook()
    kwst = each(lambda x, w: bf(stack(x * w)), k, w_end)
    bwst = each(lambda x, w: bf(stack(x * w)), bv, w_end)

    lhs = each(lambda a, b: bf(jnp.concatenate([a, b], axis=0)), at, rt)
    res_b = each(lambda a, b: _dot(a, b, NT), lhs, bst)
    res_k = each(lambda a, b: _dot(a, b, NT), lhs, kst)
    run_hook()
    a_ab = each(lambda x: jnp.where(strict, x[:CHUNK], 0.0), res_b)
    a_rb = each(lambda x: bf(jnp.where(incl, x[CHUNK:], 0.0)), res_b)
    a_ak = each(lambda x: bf(jnp.where(strict, x[:CHUNK], 0.0)), res_k)
    a_rk = each(lambda x: bf(jnp.where(incl, x[CHUNK:], 0.0)), res_k)

    n_bd = each(stack, a_ab)
    tinv = each(lambda x: eye + x, n_bd)
    pw = each(bf, n_bd)
    while hooks:
        run_hook()
    for _ in range(5):
        pw = each(lambda x: bf(_dot(x, x)), pw)
        tinv = each(lambda t, p: t + _dot(bf(t), p), tinv, pw)

    x0 = each(_dot, a_ak, vst)
    rhs = each(lambda a, x: bf(jnp.concatenate([stack(a), stack(x)], axis=1)), at, x0)
    zf = each(lambda t, x: _dot(bf(t), x), tinv, rhs)
    z = each(bf, zf)
    z_t = each(lambda x: bf(jnp.concatenate([x[:, :LANES].T, x[:, LANES:].T], axis=0)), zf)
    ry = each(_dot, a_rb, z)
    mg = each(_dot, z_t, bwst)
    rbar = each(lambda a, b: a + b[:, :LANES], rt, ry)
    y0 = each(lambda a, x, b: _dot(a, x) + b[:, LANES:], a_rk, vst, ry)
    mt = each(lambda t, x: jnp.where(eye > 0, jnp.exp(t), 0.0) + x[:LANES], tot, mg)
    gt = each(lambda a, x, b: _dot(a, x) + b[LANES:], vst_t, kwst, mg)
    return list(zip(rbar, y0, mt, gt))


def _rwkv_kernel(p_ref, mu_ref, w0_ref, wlw_ref, wla_ref, a0_ref, wg_ref, kk_ref, ka_ref, rk_ref,
                 lng_ref, lnb_ref, o_ref, carry_ref, st_ref, *, chunks_per_step):
    tb = p_ref.shape[0]
    wd = o_ref.shape[1]
    npair = wd // LANES
    rows = chunks_per_step * CHUNK

    @pl.when(pl.program_id(1) == 0)
    def _():
        carry_ref[...] = jnp.zeros_like(carry_ref)
        st_ref[...] = jnp.zeros_like(st_ref)

    row = lax.broadcasted_iota(jnp.int32, (CHUNK, LANES), 0)
    lane = lax.broadcasted_iota(jnp.int32, (CHUNK, LANES), 1)
    mask0 = lane < HEAD
    src = lane & (HEAD - 1)
    r2 = lax.broadcasted_iota(jnp.int32, (2 * CHUNK, LANES), 0)
    l2 = lax.broadcasted_iota(jnp.int32, (2 * CHUNK, LANES), 1)
    eye = (r2 == l2).astype(F32)
    lr = lax.broadcasted_iota(jnp.int32, (CHUNK, CHUNK), 0)
    lc = lax.broadcasted_iota(jnp.int32, (CHUNK, CHUNK), 1)
    ltri = (lc <= lr).astype(BF16)
    consts = (ltri, mask0, src < row, src <= row, eye)
    e_pair = ((r2 < HEAD) == (l2 < HEAD)).astype(BF16)

    def headsum(x):
        return jnp.concatenate(
            [_dot(x[:, j * LANES:(j + 1) * LANES].astype(BF16), e_pair) for j in range(npair)], axis=1)

    pair_cols = [slice(j * LANES, (j + 1) * LANES) for j in range(npair)]
    chunk_rows = [slice(i * CHUNK, (i + 1) * CHUNK) for i in range(chunks_per_step)]

    def prepare(r0):
        cur = p_ref[r0:r0 + rows, :]
        prev = pltpu.roll(cur, 1, axis=0)
        prow = lax.broadcasted_iota(jnp.int32, (8, cur.shape[1]), 0)
        prev = jnp.concatenate([jnp.where(prow == 0, carry_ref[...], prev[:8]), prev[8:]], axis=0)
        carry_ref[...] = cur[rows - 1:rows, :]
        pm = cur + (prev - cur) * mu_ref[...]

        r = pm[:, 0:wd]
        k = pm[:, wd:2 * wd]
        v = pm[:, 2 * wd:3 * wd]
        lora = pm[:, 3 * wd:3 * wd + LANES]
        xg = pm[:, 3 * wd + LANES:]
        lw = _dot(jnp.tanh(lora).astype(BF16), wlw_ref[...])
        la = _dot(lora.astype(BF16), wla_ref[...])
        g = _dot(jax.nn.sigmoid(xg).astype(BF16), wg_ref[...])
        z = w0_ref[...] + lw
        w = -(jnp.maximum(-z, 0.0) + jnp.log1p(jnp.exp(-jnp.abs(z)))) - 0.5
        ld = -jnp.exp(w)
        a = jax.nn.sigmoid(a0_ref[...] + la)
        kk = k * kk_ref[...]
        kk = kk / jnp.maximum(jnp.sqrt(headsum(kk * kk)), 1e-12)
        k2 = k * (1.0 + (a - 1.0) * ka_ref[...])
        bonus = headsum(r * k2 * rk_ref[...]) * v
        av = -kk
        bv = kk * a

        items = [(r[rs, sl], ld[rs, sl], k2[rs, sl], v[rs, sl], av[rs, sl], bv[rs, sl])
                 for sl in pair_cols for rs in chunk_rows]
        return items, bonus, g

    def state_steps(maps, bonus, g, r0):
        bf = lambda x: x.astype(BF16)
        sts, ys = [None] * npair, [[] for _ in range(npair)]

        def chunk_step(i):
            for j in range(npair):
                rbar, y0, mt, gt = maps[j * chunks_per_step + i]
                s_hi, s_lo = _split(st_ref[j] if i == 0 else sts[j])
                ys[j].append(_dot(bf(rbar), s_hi, NT) + y0)
                mtb = bf(mt)
                sts[j] = (_dot(s_lo, mtb) + _dot(s_hi, mtb)) + gt
                if i == chunks_per_step - 1:
                    st_ref[j] = sts[j]

        def output():
            y = jnp.concatenate([jnp.concatenate(c, axis=0) for c in ys], axis=1)
            mean = headsum(y) * (1.0 / HEAD)
            yc = y - mean
            var = headsum(yc * yc) * (1.0 / HEAD)
            yn = yc * lax.rsqrt(var + RW_LN_EPS) * lng_ref[...] + lnb_ref[...]
            o_ref[r0:r0 + rows, :] = (yn + bonus) * g

        return [partial(chunk_step, i) for i in range(chunks_per_step)] + [output]

    pending = []
    for r0 in range(0, tb, rows):
        items, bonus, g = prepare(r0)
        maps = _chunk_maps(items, consts, hooks=pending)
        pending = state_steps(maps, bonus, g, r0)
    for step in pending:
        step()


def _rwkv(prw, params, batch, seq, tb):
    n_tok, ncol = prw.shape
    wd = params[1].shape[1]
    nt = seq // tb
    const = lambda a: pl.BlockSpec(a.shape, lambda b, t: (0, 0), pipeline_mode=pl.Buffered(1))
    return pl.pallas_call(
        partial(_rwkv_kernel, chunks_per_step=RW_CHUNKS_PER_STEP),
        grid=(batch, nt),
        in_specs=[pl.BlockSpec((tb, ncol), lambda b, t: (b * nt + t, 0))] + [const(a) for a in params],
        out_specs=pl.BlockSpec((tb, wd), lambda b, t: (b * nt + t, 0)),
        out_shape=jax.ShapeDtypeStruct((n_tok, wd), F32),
        scratch_shapes=[pltpu.VMEM((1, ncol), F32), pltpu.VMEM((wd // LANES, LANES, LANES), F32)],
        compiler_params=pltpu.CompilerParams(dimension_semantics=("arbitrary", "arbitrary"),
                                             vmem_limit_bytes=VMEM_LIMIT),
        name="rwkv",
    )(prw, *params)


def _attn_kernel(cur_ref, prev_ref, o_ref, l_ref, *, dil, slopes):
    qb = cur_ref.shape[1]
    npr = o_ref.shape[0]
    span = ATT_L * dil
    n = pl.program_id(1)
    qi = lax.broadcasted_iota(jnp.int32, (ATT_L, 2 * ATT_L), 0)
    kj = lax.broadcasted_iota(jnp.int32, (ATT_L, 2 * ATT_L), 1)
    steps = qi + ATT_L - kj
    band = (steps >= 0) & (steps <= ATT_L)
    first = band & (kj >= jnp.where(n > 0, 0, ATT_L))
    stepsf = steps.astype(F32)
    lane = lax.broadcasted_iota(jnp.int32, (ATT_L, LANES), 1)
    mask0 = lane < HEAD
    neg = jnp.float32(-1e30)
    bias_first = [jnp.where(first, -sl * stepsf, neg) for sl in slopes]
    bias_band = [jnp.where(band, -sl * stepsf, neg) for sl in slopes] if qb > span else None
    for s in range(qb // span):
        for r in range(dil):
            sel = lambda base: pl.ds(base + r, ATT_L, stride=dil) if dil > 1 else pl.ds(base, ATT_L)
            rows = sel(s * span)
            for pr in range(npr):
                if s == 0:
                    kprev, vprev = prev_ref[npr + pr, sel(0), :], prev_ref[2 * npr + pr, sel(0), :]
                    bias = bias_first
                else:
                    prows = sel((s - 1) * span)
                    kprev, vprev = cur_ref[npr + pr, prows, :], cur_ref[2 * npr + pr, prows, :]
                    bias = bias_band
                kp = jnp.concatenate([kprev, cur_ref[npr + pr, rows, :]], axis=0).astype(BF16)
                vp = jnp.concatenate([vprev, cur_ref[2 * npr + pr, rows, :]], axis=0).astype(BF16)
                qp = cur_ref[pr, rows, :]
                o_pair = jnp.zeros((ATT_L, LANES), F32)
                l_pair = jnp.zeros((ATT_L, LANES), F32)
                for hh in range(2):
                    hmask = mask0 if hh == 0 else jnp.logical_not(mask0)
                    qm = jnp.where(hmask, qp, 0.0).astype(BF16)
                    logits = _dot(qm, kp, NT) * (HEAD ** -0.5) + bias[2 * pr + hh]
                    m = jnp.max(logits, axis=1, keepdims=True)
                    e = jnp.exp(logits - m)
                    den = jnp.sum(e, axis=1, keepdims=True)
                    o_h = _dot(e.astype(BF16), vp) / den
                    o_pair = jnp.where(hmask, o_h, o_pair)
                    l_pair = jnp.where(hmask, m + jnp.log(den), l_pair)
                o_ref[pr, rows, :] = o_pair
                l_ref[pr, rows, :] = l_pair


def _attn_group(qkv, batch, seq, dil, slopes):
    nsl, n_tok, _ = qkv.shape
    npr = nsl // 3
    span = ATT_L * dil
    qb = max(span, 512)
    nt = seq // qb
    per = qb // span
    cur = pl.BlockSpec((nsl, qb, LANES), lambda b, n: (0, b * nt + n, 0))
    prev = pl.BlockSpec((nsl, span, LANES), lambda b, n: (0, b * (seq // span) + jnp.maximum(n * per - 1, 0), 0))
    out = pl.BlockSpec((npr, qb, LANES), lambda b, n: (0, b * nt + n, 0))
    return pl.pallas_call(
        partial(_attn_kernel, dil=dil, slopes=tuple(float(s) * dil for s in slopes)),
        grid=(batch, nt),
        in_specs=[cur, prev],
        out_specs=[out, out],
        out_shape=[jax.ShapeDtypeStruct((npr, n_tok, LANES), F32)] * 2,
        compiler_params=pltpu.CompilerParams(dimension_semantics=("arbitrary",) * 2, vmem_limit_bytes=VMEM_LIMIT),
        name=f"attn_d{dil}",
    )(qkv, qkv)


def _merge_kernel(x_ref, ya_ref, o1_ref, l1_ref, o2_ref, l2_ref, o3_ref, l3_ref,
                  gmix_ref, wgate_ref, bgate_ref, wa_ref, wb_ref, wout_ref, out_ref):
    d = x_ref.shape[1]
    x = x_ref[...]
    h = _rms(x, gmix_ref[...]).astype(BF16)
    gates = jax.nn.sigmoid(_dot(h, wgate_ref[...]) + bgate_ref[...])
    cat = lambda ref: jnp.concatenate([ref[c] for c in range(ref.shape[0])], axis=1)
    l1, l2, l3 = cat(l1_ref), cat(l2_ref), cat(l3_ref)
    lm = jnp.maximum(jnp.maximum(l1, l2), l3)
    e1, e2, e3 = jnp.exp(l1 - lm), jnp.exp(l2 - lm), jnp.exp(l3 - lm)
    yb = (e1 * cat(o1_ref) + e2 * cat(o2_ref) + e3 * cat(o3_ref)) / (e1 + e2 + e3)
    merged = (gates[:, :d] * _dot(ya_ref[...].astype(BF16), wa_ref[...])
              + gates[:, d:] * _dot(yb.astype(BF16), wb_ref[...]))
    out_ref[...] = x + _dot(merged.astype(BF16), wout_ref[...])


def _merge(x2, ya, att, weights, tm):
    n_tok, d = x2.shape
    tok = lambda a: pl.BlockSpec((tm, a.shape[1]), lambda i: (i, 0))
    slab = lambda a: pl.BlockSpec((a.shape[0], tm, LANES), lambda i: (0, i, 0))
    const = lambda a: pl.BlockSpec(a.shape, lambda i: (0, 0), pipeline_mode=pl.Buffered(1))
    return pl.pallas_call(
        _merge_kernel,
        grid=(n_tok // tm,),
        in_specs=[tok(x2), tok(ya)] + [slab(a) for a in att] + [const(w) for w in weights],
        out_specs=tok(x2),
        out_shape=jax.ShapeDtypeStruct((n_tok, d), F32),
        compiler_params=pltpu.CompilerParams(dimension_semantics=("arbitrary",), vmem_limit_bytes=VMEM_LIMIT),
        name="merge",
    )(x2, ya, *att, *weights)


def _ffn_kernel(x_ref, p_ref, gffn_ref, wup_ref, cw_ref, cb_ref, wdown_ref, gple_ref, wpg_ref, wple_ref, gfin_ref,
                out_ref, ucarry_ref, act_ref, *, ff_chunk):
    tm, d = x_ref.shape
    dff = wdown_ref.shape[0]

    @pl.when(pl.program_id(1) == 0)
    def _():
        ucarry_ref[...] = jnp.zeros_like(ucarry_ref)

    x = x_ref[...]
    h = _rms(x, gffn_ref[...]).astype(BF16)
    row8 = lax.broadcasted_iota(jnp.int32, (8, ff_chunk), 0)

    def conv(u, cols):
        tail = ucarry_ref[:, cols]
        ucarry_ref[:, cols] = u[tm - 8:tm, :]
        r1, r2 = pltpu.roll(u, 1, axis=0), pltpu.roll(u, 2, axis=0)
        top1 = jnp.where(row8 == 0, tail[7:8, :], r1[:8])
        top2 = jnp.where(row8 == 0, tail[6:7, :], jnp.where(row8 == 1, tail[7:8, :], r2[:8]))
        s1 = jnp.concatenate([top1, r1[8:]], axis=0)
        s2 = jnp.concatenate([top2, r2[8:]], axis=0)
        return cb_ref[:, cols] + cw_ref[0:1, cols] * u + cw_ref[1:2, cols] * s1 + cw_ref[2:3, cols] * s2

    def up(c0):
        return _dot(h, wup_ref[:, c0:c0 + ff_chunk]), _dot(h, wup_ref[:, dff + c0:dff + c0 + ff_chunk])

    c1 = np.float32(np.sqrt(2.0 / np.pi))
    c3 = np.float32(np.sqrt(2.0 / np.pi) * 0.044715)
    starts = list(range(0, dff, ff_chunk))
    ahead = [up(c0) for c0 in starts[:FF_AHEAD]]
    for i, c0 in enumerate(starts):
        ug, uv = ahead.pop(0)
        if i + FF_AHEAD < len(starts):
            ahead.append(up(starts[i + FF_AHEAD]))
        gcols = slice(c0, c0 + ff_chunk)
        gate = conv(ug, gcols)
        val = conv(uv, slice(dff + c0, dff + c0 + ff_chunk))
        hv = 0.5 * gate * val
        act = hv + hv * jnp.tanh(gate * (c1 + c3 * (gate * gate)))
        act_ref[:, gcols] = act.astype(BF16)
    x = x + _dot(act_ref[...], wdown_ref[...])

    h = _rms(x, gple_ref[...]).astype(BF16)
    pg = jax.nn.sigmoid(_dot(h, wpg_ref[...]))
    x = x + pg * _dot(p_ref[...].astype(BF16), wple_ref[...])
    out_ref[...] = _rms(x, gfin_ref[...])


def _ffn(x2, p2, weights, batch, seq, tm, ff_chunk):
    n_tok, d = x2.shape
    nt = seq // tm
    tok = lambda a: pl.BlockSpec((tm, a.shape[1]), lambda b, t: (b * nt + t, 0))
    const = lambda a: pl.BlockSpec(a.shape, lambda b, t: (0, 0), pipeline_mode=pl.Buffered(1))
    n_up = weights[1].shape[1]
    return pl.pallas_call(
        partial(_ffn_kernel, ff_chunk=ff_chunk),
        grid=(batch, nt),
        in_specs=[tok(x2), tok(p2)] + [const(w) for w in weights],
        out_specs=tok(x2),
        out_shape=jax.ShapeDtypeStruct((n_tok, d), F32),
        scratch_shapes=[pltpu.VMEM((8, n_up), F32), pltpu.VMEM((tm, n_up // 2), BF16)],
        compiler_params=pltpu.CompilerParams(dimension_semantics=("arbitrary", "arbitrary"),
                                             vmem_limit_bytes=VMEM_LIMIT),
        name="ffn",
    )(x2, p2, *weights)


def _layer(x2, p2, batch, seq, g_mix, w_in, rw_mu, rw_w0, rw_w_up, rw_a0, rw_a_up, rw_g_up, rw_k_k, rw_k_a, rw_r_k,
           rw_ln_g, rw_ln_b, w_branch_a, w_branch_b, w_gate, b_gate, w_out, g_ffn, w_up, conv_w, conv_b,
           w_down, g_ple, w_ple_gate, w_ple, g_out):
    d = x2.shape[1]
    rw_w = rw_w0.shape[0]
    n_dec, n_aaa, n_gate = rw_w_up.shape[0], rw_a_up.shape[0], rw_g_up.shape[0]
    rw_cols = 3 * rw_w + n_dec + n_aaa + n_gate
    att_w = (w_in.shape[1] - rw_cols) // 3
    grp_w = att_w // len(ATT_GROUPS)
    assert n_dec + n_aaa == LANES and n_gate <= 2 * LANES
    row = lambda a: a.reshape(1, -1)

    gpad = 2 * LANES - n_gate
    wrw = jnp.pad(w_in[:, :rw_cols], ((0, 0), (0, gpad))).astype(BF16)
    mu = jnp.pad(rw_mu, (0, gpad)).reshape(1, -1)
    watt = []
    for gi in range(len(ATT_GROUPS)):
        cols = [w_in[:, rw_cols + part * att_w + gi * grp_w: rw_cols + part * att_w + (gi + 1) * grp_w] for part in range(3)]
        watt.append(jnp.concatenate(cols, axis=1).astype(BF16))
    wlw = jnp.pad(rw_w_up, ((0, n_aaa), (0, 0))).astype(BF16)
    wla = jnp.pad(rw_a_up, ((n_dec, 0), (0, 0))).astype(BF16)
    wg = jnp.pad(rw_g_up, ((0, gpad), (0, 0))).astype(BF16)

    prw, q1, q2, q3 = _in_proj(x2, row(g_mix), wrw, watt, tm=TOKEN_TILE)
    rw_params = (mu, row(rw_w0), wlw, wla, row(rw_a0), wg, row(rw_k_k), row(rw_k_a), row(rw_r_k),
                 row(rw_ln_g), row(rw_ln_b))
    ya = _rwkv(prw, rw_params, batch, seq, tb=RW_TOKEN_TILE)

    n_heads = len(ATT_GROUPS) * grp_w // HEAD
    slopes = [2.0 ** (-8.0 * (h + 1) / n_heads) for h in range(n_heads)]
    att = []
    for gi, (qkv, (window, dil)) in enumerate(zip((q1, q2, q3), ATT_GROUPS)):
        assert window // dil == ATT_L
        hpg = grp_w // HEAD
        att.extend(_attn_group(qkv, batch, seq, dil, slopes[gi * hpg:(gi + 1) * hpg]))

    merge_w = (row(g_mix), w_gate.astype(BF16), row(b_gate), w_branch_a.astype(BF16), w_branch_b.astype(BF16),
               w_out.astype(BF16))
    x2 = _merge(x2, ya, att, merge_w, tm=TOKEN_TILE)
    ffn_w = (row(g_ffn), w_up.astype(BF16), conv_w, row(conv_b), w_down.astype(BF16),
             row(g_ple), w_ple_gate.astype(BF16), w_ple.astype(BF16), row(g_out))
    return _ffn(x2, p2, ffn_w, batch, seq, tm=TOKEN_TILE, ff_chunk=FF_CHUNK)


def kernel(x, p, g_mix, w_in, rw_mu, rw_w0, rw_w_up, rw_a0, rw_a_up, rw_g_up, rw_k_k, rw_k_a, rw_r_k, rw_ln_g, rw_ln_b, w_branch_a, w_branch_b, w_gate, b_gate, w_out, g_ffn, w_up, conv_w, conv_b, w_down, g_ple, w_ple_gate, w_ple, g_final):
    batch, seq, d = x.shape
    depth = w_in.shape[0]
    x2 = x.reshape(batch * seq, d)
    per_layer = (g_mix, w_in, rw_mu, rw_w0, rw_w_up, rw_a0, rw_a_up, rw_g_up, rw_k_k, rw_k_a, rw_r_k, rw_ln_g,
                 rw_ln_b, w_branch_a, w_branch_b, w_gate, b_gate, w_out, g_ffn, w_up, conv_w, conv_b, w_down,
                 g_ple, w_ple_gate, w_ple)
    for i in range(depth):
        assert depth == 1
        x2 = _layer(x2, p[i].reshape(batch * seq, -1), batch, seq, *[w[i] for w in per_layer], g_final)
    return x2.reshape(batch, seq, d)
```
